```python
import math
import jax, jax.numpy as jnp
from jax import lax
import numpy as np

D_MODEL = 2048
BATCH = 16
SEQ = 2048
DEPTH = 4

CHUNK = 64
N_A_LAYERS = DEPTH // 2
N_B_LAYERS = DEPTH - N_A_LAYERS
SSM_WIDTH = D_MODEL
SSM_GROUP = 16
SSM_GROUPS = SSM_WIDTH // SSM_GROUP
SSM_STATE = 64
HEAD_DIM = 128
N_HEADS = D_MODEL // HEAD_DIM
N_KV_HEADS = 4
Q_PER_KV = N_HEADS // N_KV_HEADS
Q_BLOCK = 128
D_FF = 4 * D_MODEL
PLE_DIM = 256
NORM_EPS = 1e-6
DT_MIN = 1e-3
DT_MAX = 1e-1

kernel_name = "yoco_s5_stickbreaking_hybrid"


def rms_norm(x, g):
    x32 = x.astype(jnp.float32)
    y = x32 * lax.rsqrt(jnp.mean(x32 * x32, axis=-1, keepdims=True) + NORM_EPS)
    return (y * g.astype(jnp.float32)).astype(x.dtype)


def _complex_affine_combine(e1, e2):
    a1r, a1i, b1r, b1i = e1
    a2r, a2i, b2r, b2i = e2
    ar = a2r * a1r - a2i * a1i
    ai = a2r * a1i + a2i * a1r
    br = a2r * b1r - a2i * b1i + b2r
    bi = a2r * b1i + a2i * b1r + b2i
    return (ar, ai, br, bi)


def s5_mixer(h, w_in, a_re, a_im, log_step, b_re, b_im, c_re, c_im, d_skip, w_glu):
    f32 = jnp.float32
    bsz, seq, _ = h.shape
    u = (h @ w_in).astype(f32).reshape(bsz, seq, SSM_GROUPS, SSM_GROUP)
    dt = jnp.exp(log_step.astype(f32))[:, None]
    ar = a_re.astype(f32)
    ai = a_im.astype(f32)
    mag = jnp.exp(ar * dt)
    lbr = mag * jnp.cos(ai * dt)
    lbi = mag * jnp.sin(ai * dt)
    den = ar * ar + ai * ai
    nr = lbr - 1.0
    fr = (nr * ar + lbi * ai) / den
    fi = (lbi * ar - nr * ai) / den
    br = b_re.astype(f32)
    bi = b_im.astype(f32)
    bbr = fr[..., None] * br - fi[..., None] * bi
    bbi = fr[..., None] * bi + fi[..., None] * br
    bu_re = jnp.einsum('bsgh,gph->bsgp', u, bbr)
    bu_im = jnp.einsum('bsgh,gph->bsgp', u, bbi)
    lam_re = jnp.broadcast_to(lbr[None, None], bu_re.shape)
    lam_im = jnp.broadcast_to(lbi[None, None], bu_im.shape)
    _, _, xs_re, xs_im = lax.associative_scan(
        _complex_affine_combine, (lam_re, lam_im, bu_re, bu_im), axis=1)
    y = (jnp.einsum('bsgp,ghp->bsgh', xs_re, c_re.astype(f32))
         - jnp.einsum('bsgp,ghp->bsgh', xs_im, c_im.astype(f32))
         + d_skip.astype(f32).reshape(SSM_GROUPS, SSM_GROUP) * u)
    y = jax.nn.gelu(y.reshape(bsz, seq, SSM_WIDTH)).astype(h.dtype)
    val, gate = jnp.split(y @ w_glu, 2, axis=-1)
    return val * jax.nn.sigmoid(gate)


def stick_breaking_attention(q, k, v):
    seq = q.shape[1]
    scale = HEAD_DIM ** -0.5
    outs = []
    for blk in range(seq // Q_BLOCK):
        q0 = blk * Q_BLOCK
        end = q0 + Q_BLOCK
        qb = q[:, q0:end]
        kb = k[:, :end]
        vb = v[:, :end]
        z = jnp.einsum('bqhgd,bkhd->bhgqk', qb, kb,
                       preferred_element_type=jnp.float32) * scale
        t_pos = q0 + jnp.arange(Q_BLOCK)[:, None]
        s_pos = jnp.arange(end)[None, :]
        strict = s_pos < t_pos
        log_keep = jnp.where(strict, jax.nn.log_sigmoid(-z), 0.0)
        after = lax.cumsum(log_keep, axis=log_keep.ndim - 1, reverse=True) - log_keep
        w = jnp.where(strict, jnp.exp(jax.nn.log_sigmoid(z) + after), 0.0)
        outs.append(jnp.einsum('bhgqk,bkhd->bqhgd', w.astype(vb.dtype), vb))
    return jnp.concatenate(outs, axis=1)


def setup_inputs(seed: int = 0) -> dict:
    key = jax.random.key(seed)
    ks = jax.random.split(key, 32)
    f32 = jnp.float32
    D, H, G, P, Hg = D_MODEL, SSM_WIDTH, SSM_GROUPS, SSM_STATE, SSM_GROUP
    nA, nB = N_A_LAYERS, N_B_LAYERS
    kvw = N_KV_HEADS * HEAD_DIM
    qw = N_HEADS * HEAD_DIM

    def nrm(k, shape, scale):
        return jax.random.normal(k, shape, f32) * scale

    def gain(k, shape):
        return 1.0 + 0.01 * jax.random.normal(k, shape, f32)

    n_idx = jnp.arange(P, dtype=f32)
    return {
        "x": nrm(ks[0], (BATCH, SEQ, D), 1.0),
        "p": nrm(ks[1], (DEPTH, BATCH, SEQ, PLE_DIM), 1.0),
        "a_norm_pre": gain(ks[2], (nA, D)),
        "a_norm_post": gain(ks[3], (nA, D)),
        "ssm_w_in": nrm(ks[4], (nA, D, H), D ** -0.5),
        "ssm_a_re": -0.5 + 0.01 * jax.random.normal(ks[5], (nA, G, P), f32),
        "ssm_a_im": math.pi * n_idx + 0.01 * jax.random.normal(ks[6], (nA, G, P), f32),
        "ssm_log_step": jax.random.uniform(ks[7], (nA, G), f32,
                                           minval=math.log(DT_MIN), maxval=math.log(DT_MAX)),
        "ssm_b_re": nrm(ks[8], (nA, G, P, Hg), (2 * Hg) ** -0.5),
        "ssm_b_im": nrm(ks[9], (nA, G, P, Hg), (2 * Hg) ** -0.5),
        "ssm_c_re": nrm(ks[10], (nA, G, Hg, P), (2 * P) ** -0.5),
        "ssm_c_im": nrm(ks[11], (nA, G, Hg, P), (2 * P) ** -0.5),
        "ssm_d": nrm(ks[12], (nA, H), 0.5),
        "ssm_w_glu": nrm(ks[13], (nA, H, 2 * D), H ** -0.5),
        "kv_norm": gain(ks[14], (D,)),
        "w_k": nrm(ks[15], (D, kvw), D ** -0.5),
        "w_v": nrm(ks[16], (D, kvw), D ** -0.5),
        "b_norm_pre": gain(ks[17], (nB, D)),
        "b_norm_post": gain(ks[18], (nB, D)),
        "w_q": nrm(ks[19], (nB, D, qw), D ** -0.5),
        "w_o": nrm(ks[20], (nB, qw, D), qw ** -0.5),
        "mlp_norm_pre": gain(ks[21], (DEPTH, D)),
        "mlp_norm_post": gain(ks[22], (DEPTH, D)),
        "mlp_w1": nrm(ks[23], (DEPTH, D, D_FF), D ** -0.5),
        "mlp_w2": nrm(ks[24], (DEPTH, D_FF, D), D_FF ** -0.5),
        "ple_w": nrm(ks[25], (DEPTH, PLE_DIM, D), PLE_DIM ** -0.5),
        "ple_gate": nrm(ks[26], (DEPTH, D, D), D ** -0.5),
    }


def reference(x, p, a_norm_pre, a_norm_post, ssm_w_in, ssm_a_re, ssm_a_im, ssm_log_step,
              ssm_b_re, ssm_b_im, ssm_c_re, ssm_c_im, ssm_d, ssm_w_glu,
              kv_norm, w_k, w_v, b_norm_pre, b_norm_post, w_q, w_o,
              mlp_norm_pre, mlp_norm_post, mlp_w1, mlp_w2, ple_w, ple_gate):
    bsz, seq, _ = x.shape
    h = x
    k_sh = None
    v_sh = None
    for i in range(DEPTH):
        if i < N_A_LAYERS:
            j = i
            m = s5_mixer(rms_norm(h, a_norm_pre[j]), ssm_w_in[j], ssm_a_re[j], ssm_a_im[j],
                         ssm_log_step[j], ssm_b_re[j], ssm_b_im[j], ssm_c_re[j], ssm_c_im[j],
                         ssm_d[j], ssm_w_glu[j])
            h = h + rms_norm(m, a_norm_post[j])
        else:
            j = i - N_A_LAYERS
            if j == 0:
                kv_in = rms_norm(h, kv_norm)
                k_sh = (kv_in @ w_k).reshape(bsz, seq, N_KV_HEADS, HEAD_DIM)
                v_sh = (kv_in @ w_v).reshape(bsz, seq, N_KV_HEADS, HEAD_DIM)
            q = (rms_norm(h, b_norm_pre[j]) @ w_q[j]).reshape(
                bsz, seq, N_KV_HEADS, Q_PER_KV, HEAD_DIM)
            o = stick_breaking_attention(q, k_sh, v_sh).reshape(bsz, seq, N_HEADS * HEAD_DIM)
            h = h + rms_norm(o @ w_o[j], b_norm_post[j])
        f = rms_norm(h, mlp_norm_pre[i]) @ mlp_w1[i]
        f = jnp.square(jax.nn.relu(f)) @ mlp_w2[i]
        h = h + rms_norm(f, mlp_norm_post[i])
        h = h + (p[i] @ ple_w[i]) * jax.nn.sigmoid(h @ ple_gate[i])
    return h
```

```python
import functools
import math

import jax
import jax.numpy as jnp
from jax import lax
from jax.experimental import pallas as pl
from jax.experimental.pallas import tpu as pltpu

F32 = jnp.float32
BF16 = jnp.bfloat16

NORM_EPS = 1e-6
SSM_GROUP = 16
SSM_STATE = 64
HEAD_DIM = 128
N_KV_HEADS = 4
Q_PER_KV = 4

LANES = 128
SLAB_GROUPS = LANES // SSM_GROUP
SLAB_STATE = SLAB_GROUPS * SSM_STATE
V7X_VMEM_BYTES = 64 * 1024 * 1024
VMEM_CAP = V7X_VMEM_BYTES - 8 * 1024 * 1024

ROW_TILE = 512
FF_TILE = 512
S5_STEPS = 64
ATT_TILE = 128
ATT_SKIP = 60.0


def _vmem_limit(nbytes):
    return int(min(VMEM_CAP, nbytes * 5 // 4 + (4 << 20)))


def _rms(x, g):
    ms = jnp.mean(x * x, axis=-1, keepdims=True)
    return x * lax.rsqrt(ms + NORM_EPS) * g


def _norm_proj_kernel(x_ref, g_ref, w_ref, o_ref):
    xn = _rms(x_ref[...], g_ref[...]).astype(BF16)
    o_ref[...] = jnp.dot(xn, w_ref[...], preferred_element_type=F32).astype(o_ref.dtype)


def _norm_proj(h, g, w, *, time_major, name):
    b, s, d = h.shape
    n = w.shape[1]
    ts = min(ROW_TILE, s)
    if time_major:
        out_shape = jax.ShapeDtypeStruct((s, b * n), BF16)
        out_spec = pl.BlockSpec((ts, n), lambda bi, si: (si, bi))
    else:
        out_shape = jax.ShapeDtypeStruct((b, s, n), BF16)
        out_spec = pl.BlockSpec((None, ts, n), lambda bi, si: (bi, si, 0))
    est = 2 * ts * d * 4 + d * n * 2 + 2 * ts * n * 2 + ts * d * 2 + ts * n * 4 + ts * d * 4
    return pl.pallas_call(
        _norm_proj_kernel,
        grid=(b, s // ts),
        in_specs=[
            pl.BlockSpec((None, ts, d), lambda bi, si: (bi, si, 0)),
            pl.BlockSpec((1, d), lambda bi, si: (0, 0)),
            pl.BlockSpec((d, n), lambda bi, si: (0, 0), pipeline_mode=pl.Buffered(1)),
        ],
        out_specs=out_spec,
        out_shape=out_shape,
        compiler_params=pltpu.CompilerParams(
            dimension_semantics=("parallel", "parallel"),
            vmem_limit_bytes=_vmem_limit(est)),
        name=name,
    )(h, g.reshape(1, d), w)


def _proj_post_kernel(y_ref, w_ref, g_ref, h_ref, o_ref, m_ref, *, glu, n_chunk):
    y = y_ref[...]
    d = h_ref.shape[-1]
    for c in range(d // n_chunk):
        lo = c * n_chunk
        val = jnp.dot(y, w_ref[:, lo:lo + n_chunk], preferred_element_type=F32)
        if glu:
            gate = jnp.dot(y, w_ref[:, d + lo:d + lo + n_chunk], preferred_element_type=F32)
            val = val * jax.nn.sigmoid(gate)
        m_ref[:, lo:lo + n_chunk] = val
    o_ref[...] = h_ref[...] + _rms(m_ref[...], g_ref[...])


def _proj_post(y, w, g, h, *, glu, time_major, name):
    b, s, d = h.shape
    k, n = w.shape
    ts = min(ROW_TILE, s)
    if time_major:
        y_spec = pl.BlockSpec((ts, k), lambda bi, si: (si, bi))
    else:
        y_spec = pl.BlockSpec((None, ts, k), lambda bi, si: (bi, si, 0))
    h_spec = pl.BlockSpec((None, ts, d), lambda bi, si: (bi, si, 0))
    n_chunk = 512
    est = 2 * ts * k * 2 + k * n * 2 + 4 * ts * d * 4 + ts * d * 4 + 4 * ts * n_chunk * 4 + ts * d * 4
    return pl.pallas_call(
        functools.partial(_proj_post_kernel, glu=glu, n_chunk=n_chunk),
        grid=(b, s // ts),
        in_specs=[
            y_spec,
            pl.BlockSpec((k, n), lambda bi, si: (0, 0), pipeline_mode=pl.Buffered(1)),
            pl.BlockSpec((1, d), lambda bi, si: (0, 0)),
            h_spec,
        ],
        out_specs=h_spec,
        out_shape=jax.ShapeDtypeStruct(h.shape, F32),
        scratch_shapes=[pltpu.VMEM((ts, d), F32)],
        compiler_params=pltpu.CompilerParams(
            dimension_semantics=("parallel", "parallel"),
            vmem_limit_bytes=_vmem_limit(est)),
        name=name,
    )(y, w, g.reshape(1, d), h)


def _mlp_kernel(h_ref, g1_ref, w1_ref, w2_ref, g2_ref, o_ref, hn_ref, acc_ref):
    f = pl.program_id(1)

    @pl.when(f == 0)
    def _():
        hn_ref[...] = _rms(h_ref[...], g1_ref[...]).astype(BF16)
        acc_ref[...] = jnp.zeros_like(acc_ref)

    a = jnp.dot(hn_ref[...], w1_ref[...], preferred_element_type=F32)
    a = jnp.square(jnp.maximum(a, 0.0)).astype(BF16)
    acc_ref[...] += jnp.dot(a, w2_ref[...], preferred_element_type=F32)

    @pl.when(f == pl.num_programs(1) - 1)
    def _():
        o_ref[...] = h_ref[...] + _rms(acc_ref[...], g2_ref[...])


def _mlp(h, g1, w1, w2, g2, *, name):
    b, s, d = h.shape
    t = b * s
    ff = w1.shape[1]
    tm = min(ROW_TILE, t)
    tf = min(FF_TILE, ff)
    est = 4 * tm * d * 4 + tm * d * 2 + tm * d * 4 + 4 * d * tf * 2 + 2 * tm * tf * 4 + tm * d * 4
    out = pl.pallas_call(
        _mlp_kernel,
        grid=(t // tm, ff // tf),
        in_specs=[
            pl.BlockSpec((tm, d), lambda m, f: (m, 0)),
            pl.BlockSpec((1, d), lambda m, f: (0, 0)),
            pl.BlockSpec((d, tf), lambda m, f: (0, f)),
            pl.BlockSpec((tf, d), lambda m, f: (f, 0)),
            pl.BlockSpec((1, d), lambda m, f: (0, 0)),
        ],
        out_specs=pl.BlockSpec((tm, d), lambda m, f: (m, 0)),
        out_shape=jax.ShapeDtypeStruct((t, d), F32),
        scratch_shapes=[pltpu.VMEM((tm, d), BF16), pltpu.VMEM((tm, d), F32)],
        compiler_params=pltpu.CompilerParams(
            dimension_semantics=("parallel", "arbitrary"),
            vmem_limit_bytes=_vmem_limit(est)),
        name=name,
    )(h.reshape(t, d), g1.reshape(1, d), w1, w2, g2.reshape(1, d))
    return out.reshape(b, s, d)


def _ple_kernel(h_ref, p_ref, wp_ref, wg_ref, o_ref):
    h = h_ref[...]
    gate = jnp.dot(h.astype(BF16), wg_ref[...], preferred_element_type=F32)
    pe = jnp.dot(p_ref[...].astype(BF16), wp_ref[...], preferred_element_type=F32)
    o_ref[...] = h + pe * jax.nn.sigmoid(gate)


def _ple(h, p_all, layer, wp, wg, *, name):
    b, s, d = h.shape
    t = b * s
    pd = p_all.shape[-1]
    tm = min(ROW_TILE, t)
    est = 4 * tm * d * 4 + 2 * tm * pd * 4 + d * d * 2 + pd * d * 2 + tm * d * 2 + 3 * tm * d * 4
    out = pl.pallas_call(
        _ple_kernel,
        grid=(t // tm,),
        in_specs=[
            pl.BlockSpec((tm, d), lambda m: (m, 0)),
            pl.BlockSpec((None, tm, pd), lambda m: (layer, m, 0)),
            pl.BlockSpec((pd, d), lambda m: (0, 0), pipeline_mode=pl.Buffered(1)),
            pl.BlockSpec((d, d), lambda m: (0, 0), pipeline_mode=pl.Buffered(1)),
        ],
        out_specs=pl.BlockSpec((tm, d), lambda m: (m, 0)),
        out_shape=jax.ShapeDtypeStruct((t, d), F32),
        compiler_params=pltpu.CompilerParams(
            dimension_semantics=("parallel",),
            vmem_limit_bytes=_vmem_limit(est)),
        name=name,
    )(h.reshape(t, d), p_all.reshape(p_all.shape[0], t, pd), wp, wg)
    return out.reshape(b, s, d)


def _s5_disc_kernel(are_ref, aim_ref, ls_ref, bre_ref, bim_ref,
                    lbr_ref, lbi_ref, bbr_ref, bbi_ref):
    ar = are_ref[...]
    ai = aim_ref[...]
    dt = jnp.exp(ls_ref[...])
    mag = jnp.exp(ar * dt)
    lbr = mag * jnp.cos(ai * dt)
    lbi = mag * jnp.sin(ai * dt)
    den = ar * ar + ai * ai
    nr = lbr - 1.0
    fr = (nr * ar + lbi * ai) / den
    fi = (lbi * ar - nr * ai) / den
    lbr_ref[...] = lbr
    lbi_ref[...] = lbi
    br = bre_ref[...]
    bi = bim_ref[...]
    bbr_ref[...] = fr[None] * br - fi[None] * bi
    bbi_ref[...] = fr[None] * bi + fi[None] * br


def _s5_discretise(a_re, a_im, log_step, b_re, b_im):
    g, p = a_re.shape
    hg = b_re.shape[-1]
    bt_re = jnp.transpose(b_re, (2, 0, 1))
    bt_im = jnp.transpose(b_im, (2, 0, 1))
    gp = jax.ShapeDtypeStruct((g, p), F32)
    hgp = jax.ShapeDtypeStruct((hg, g, p), F32)
    return pl.pallas_call(
        _s5_disc_kernel,
        out_shape=(gp, gp, hgp, hgp),
        name="s5_discretise",
    )(a_re, a_im, log_step.reshape(g, 1), bt_re, bt_im)


def _slab_block_diag(x, rows_are_state):
    del rows_are_state
    eye = jnp.eye(SLAB_GROUPS, dtype=x.dtype)
    ns, gl, a, b = x.shape
    y = jnp.einsum('sgab,gm->sgamb', x, eye)
    return y.reshape(ns, gl * a, gl * b)


def _s5_core_kernel(u_ref, b_ref, c_ref, lam_ref, d_ref, o_ref,
                    bu_ref, xs_ref, st_ref, *, n_batch, n_steps):
    ns = SLAB_STATE

    @pl.when(pl.program_id(1) == 0)
    def _():
        st_ref[...] = jnp.zeros_like(st_ref)

    u = u_ref[...]
    bu_ref[...] = jnp.dot(u, b_ref[...], preferred_element_type=F32)
    lr = jnp.broadcast_to(lam_ref[:, :ns], (n_batch, ns))
    li = jnp.broadcast_to(lam_ref[:, ns:], (n_batch, ns))

    def step(t, carry):
        xr, xi = carry
        r0 = pl.multiple_of(t * n_batch, n_batch)
        rows = pl.ds(r0, n_batch)
        nxr = lr * xr - li * xi + bu_ref[rows, :ns]
        nxi = lr * xi + li * xr + bu_ref[rows, ns:]
        xs_ref[rows, :ns] = nxr.astype(BF16)
        xs_ref[rows, ns:] = nxi.astype(BF16)
        return nxr, nxi

    xr, xi = lax.fori_loop(0, n_steps, step, (st_ref[:, :ns], st_ref[:, ns:]), unroll=4)
    st_ref[:, :ns] = xr
    st_ref[:, ns:] = xi

    y = jnp.dot(xs_ref[...], c_ref[...], preferred_element_type=F32)
    y = y + d_ref[...] * u.astype(F32)
    o_ref[...] = jax.nn.gelu(y).astype(o_ref.dtype)


def _s5_core(u_tm, b_exp, c_exp, lam, d_skip, *, n_batch):
    rows, hdim = u_tm.shape
    n_slab = hdim // LANES
    seq = rows // n_batch
    n_steps = min(S5_STEPS, seq)
    r = n_steps * n_batch
    est = (4 * r * LANES * 2 + 4 * LANES * 2 * SLAB_STATE * 2 + r * 2 * SLAB_STATE * 6
           + 2 * r * 2 * SLAB_STATE * 4)
    return pl.pallas_call(
        functools.partial(_s5_core_kernel, n_batch=n_batch, n_steps=n_steps),
        grid=(n_slab, seq // n_steps),
        in_specs=[
            pl.BlockSpec((r, LANES), lambda s, c: (c, s)),
            pl.BlockSpec((None, LANES, 2 * SLAB_STATE), lambda s, c: (s, 0, 0)),
            pl.BlockSpec((None, 2 * SLAB_STATE, LANES), lambda s, c: (s, 0, 0)),
            pl.BlockSpec((None, 1, 2 * SLAB_STATE), lambda s, c: (s, 0, 0)),
            pl.BlockSpec((None, 1, LANES), lambda s, c: (s, 0, 0)),
        ],
        out_specs=pl.BlockSpec((r, LANES), lambda s, c: (c, s)),
        out_shape=jax.ShapeDtypeStruct((rows, hdim), BF16),
        scratch_shapes=[
            pltpu.VMEM((r, 2 * SLAB_STATE), F32),
            pltpu.VMEM((r, 2 * SLAB_STATE), BF16),
            pltpu.VMEM((n_batch, 2 * SLAB_STATE), F32),
        ],
        compiler_params=pltpu.CompilerParams(
            dimension_semantics=("parallel", "arbitrary"),
            vmem_limit_bytes=_vmem_limit(est)),
        name="s5_core",
    )(u_tm, b_exp, c_exp, lam, d_skip)


def _s5_mixer(h, g_pre, w_in, a_re, a_im, log_step, b_re, b_im, c_re, c_im, d_skip,
              w_glu, g_post, *, layer):
    b, s, d = h.shape
    hdim = w_in.shape[1]
    n_slab = hdim // LANES
    lbr, lbi, bbr, bbi = _s5_discretise(a_re, a_im, log_step, b_re, b_im)

    def slabs_b(x):
        x = jnp.transpose(x, (1, 0, 2))
        return x.reshape(n_slab, SLAB_GROUPS, SSM_GROUP, SSM_STATE)

    def slabs_c(x):
        x = jnp.transpose(x, (0, 2, 1))
        return x.reshape(n_slab, SLAB_GROUPS, SSM_STATE, SSM_GROUP)

    b_exp = jnp.concatenate(
        [_slab_block_diag(slabs_b(bbr), False), _slab_block_diag(slabs_b(bbi), False)],
        axis=2).astype(BF16)
    c_exp = jnp.concatenate(
        [_slab_block_diag(slabs_c(c_re), True), _slab_block_diag(slabs_c(-c_im), True)],
        axis=1).astype(BF16)
    lam = jnp.concatenate(
        [lbr.reshape(n_slab, 1, SLAB_STATE), lbi.reshape(n_slab, 1, SLAB_STATE)], axis=2)
    dsk = d_skip.astype(F32).reshape(n_slab, 1, LANES)

    u = _norm_proj(h, g_pre, w_in, time_major=True, name=f"s5_in_proj_{layer}")
    y = _s5_core(u.reshape(s * b, hdim), b_exp, c_exp, lam, dsk, n_batch=b)
    return _proj_post(y.reshape(s, b * hdim), w_glu, g_post, h, glu=True, time_major=True,
                      name=f"s5_out_proj_{layer}")


def _attn_kernel(q_ref, k_ref, v_ref, o_ref, qs_ref, acc_ref, c_ref, *, tq, scale):
    i = pl.program_id(2)
    tk = tq
    r = Q_PER_KV * tq
    for g in range(Q_PER_KV):
        qs_ref[g * tq:(g + 1) * tq, :] = q_ref[:, g * HEAD_DIM:(g + 1) * HEAD_DIM]
    acc_ref[...] = jnp.zeros_like(acc_ref)
    c_ref[...] = jnp.zeros_like(c_ref)

    kr = lax.broadcasted_iota(jnp.int32, (tk, 2 * tk), 0)
    kc = lax.broadcasted_iota(jnp.int32, (tk, 2 * tk), 1)
    suffix = jnp.where((kr > kc) | (kc >= tk), 1.0, 0.0).astype(BF16)
    row = lax.broadcasted_iota(jnp.int32, (r, tk), 0) & (tq - 1)
    col = lax.broadcasted_iota(jnp.int32, (r, tk), 1)
    strict = col < row

    def block(j, masked):
        k0 = pl.multiple_of(j * tk, tk)
        kb = k_ref[pl.ds(k0, tk), :]
        vb = v_ref[pl.ds(k0, tk), :]
        z = lax.dot_general(qs_ref[...], kb, (((1,), (1,)), ((), ())),
                            preferred_element_type=F32) * scale
        log_beta = jnp.minimum(z, 0.0) - jnp.log(1.0 + jnp.exp(-jnp.abs(z)))
        log_keep = log_beta - z
        if masked:
            log_keep = jnp.where(strict, log_keep, 0.0)
        sums = jnp.dot(log_keep.astype(BF16), suffix, preferred_element_type=F32)
        w = jnp.exp(log_beta + sums[:, :tk] + c_ref[...])
        if masked:
            w = jnp.where(strict, w, 0.0)
        acc_ref[...] += jnp.dot(w.astype(BF16), vb, preferred_element_type=F32)
        c_ref[...] += sums[:, tk:]

    block(i, True)

    def cond(j):
        return jnp.logical_and(j >= 0, jnp.max(c_ref[...]) > -ATT_SKIP)

    def body(j):
        block(j, False)
        return j - 1

    lax.while_loop(cond, body, i - 1)

    for g in range(Q_PER_KV):
        o_ref[:, g * HEAD_DIM:(g + 1) * HEAD_DIM] = acc_ref[g * tq:(g + 1) * tq, :].astype(o_ref.dtype)


def _attention(q, kv):
    b, s, qw = q.shape
    tq = min(ATT_TILE, s)
    gw = Q_PER_KV * HEAD_DIM
    r = Q_PER_KV * tq
    return pl.pallas_call(
        functools.partial(_attn_kernel, tq=tq, scale=HEAD_DIM ** -0.5),
        grid=(b, N_KV_HEADS, s // tq),
        in_specs=[
            pl.BlockSpec((None, tq, gw), lambda bi, hi, qi: (bi, qi, hi)),
            pl.BlockSpec((None, s, HEAD_DIM), lambda bi, hi, qi: (bi, 0, hi)),
            pl.BlockSpec((None, s, HEAD_DIM), lambda bi, hi, qi: (bi, 0, N_KV_HEADS + hi)),
        ],
        out_specs=pl.BlockSpec((None, tq, gw), lambda bi, hi, qi: (bi, qi, hi)),
        out_shape=jax.ShapeDtypeStruct((b, s, qw), BF16),
        scratch_shapes=[
            pltpu.VMEM((r, HEAD_DIM), BF16),
            pltpu.VMEM((r, HEAD_DIM), F32),
            pltpu.VMEM((r, tq), F32),
        ],
        compiler_params=pltpu.CompilerParams(
            dimension_semantics=("parallel", "parallel", "arbitrary")),
        name="stick_breaking_attention",
    )(q, kv, kv)


def kernel(x, p, a_norm_pre, a_norm_post, ssm_w_in, ssm_a_re, ssm_a_im, ssm_log_step,
           ssm_b_re, ssm_b_im, ssm_c_re, ssm_c_im, ssm_d, ssm_w_glu,
           kv_norm, w_k, w_v, b_norm_pre, b_norm_post, w_q, w_o,
           mlp_norm_pre, mlp_norm_post, mlp_w1, mlp_w2, ple_w, ple_gate):
    depth = mlp_w1.shape[0]
    n_a = ssm_w_in.shape[0]

    w_in = ssm_w_in.astype(BF16)
    w_glu = ssm_w_glu.astype(BF16)
    w_kv = jnp.concatenate([w_k, w_v], axis=1).astype(BF16)
    wq = w_q.astype(BF16)
    wo = w_o.astype(BF16)
    w1 = mlp_w1.astype(BF16)
    w2 = mlp_w2.astype(BF16)
    wp = ple_w.astype(BF16)
    wg = ple_gate.astype(BF16)

    h = x
    kv = None
    for i in range(depth):
        if i < n_a:
            h = _s5_mixer(h, a_norm_pre[i], w_in[i], ssm_a_re[i], ssm_a_im[i], ssm_log_step[i],
                          ssm_b_re[i], ssm_b_im[i], ssm_c_re[i], ssm_c_im[i], ssm_d[i],
                          w_glu[i], a_norm_post[i], layer=i)
        else:
            j = i - n_a
            if j == 0:
                kv = _norm_proj(h, kv_norm, w_kv, time_major=False, name="kv_proj")
            q = _norm_proj(h, b_norm_pre[j], wq[j], time_major=False, name=f"q_proj_{j}")
            o = _attention(q, kv)
            h = _proj_post(o, wo[j], b_norm_post[j], h, glu=False, time_major=False,
                           name=f"attn_out_proj_{j}")
        h = _mlp(h, mlp_norm_pre[i], w1[i], w2[i], mlp_norm_post[i], name=f"mlp_{i}")
        h = _ple(h, p, i, wp[i], wg[i], name=f"ple_{i}")
    return h
```

```python
import functools
import math

import jax
import jax.numpy as jnp
from jax import lax
from jax.experimental import pallas as pl
from jax.experimental.pallas import tpu as pltpu

F32 = jnp.float32
BF16 = jnp.bfloat16

NORM_EPS = 1e-6
SSM_GROUP = 16
SSM_STATE = 64
HEAD_DIM = 128
N_KV_HEADS = 4
Q_PER_KV = 4

LANES = 128
SUBLANES = 8
SLAB_GROUPS = LANES // SSM_GROUP
SLAB_STATE = SLAB_GROUPS * SSM_STATE
V7X_VMEM_BYTES = 64 * 1024 * 1024
VMEM_CAP = V7X_VMEM_BYTES - 8 * 1024 * 1024

ROW_TILE = 512
GLU_ROW_TILE = 256
FF_TILE = 1024
S5_STEPS = 128
ATT_TILE = 128
LOG2E = math.log2(math.e)
ATT_SKIP = 60.0


def _vmem_limit(nbytes):
    return int(min(VMEM_CAP, nbytes * 5 // 4 + (4 << 20)))


def _rms(x, g):
    ms = jnp.mean(x * x, axis=-1, keepdims=True)
    return x * lax.rsqrt(ms + NORM_EPS) * g


def _norm_proj_kernel(x_ref, g_ref, w_ref, o_ref):
    xn = _rms(x_ref[...], g_ref[...]).astype(BF16)
    y = jnp.dot(xn, w_ref[...], preferred_element_type=F32).astype(o_ref.dtype)
    o_ref[...] = y.reshape(o_ref.shape)


def _norm_proj(h, g, w, layer, *, time_major, name):
    b, s, d = h.shape
    n = w.shape[-1]
    ts = min(ROW_TILE, s)
    if time_major:
        out_shape = jax.ShapeDtypeStruct((s // SUBLANES, b, SUBLANES, n), F32)
        out_spec = pl.BlockSpec((ts // SUBLANES, None, SUBLANES, n),
                                lambda bi, si: (si, bi, 0, 0))
        out_bytes = 4
    else:
        out_shape = jax.ShapeDtypeStruct((b, s, n), BF16)
        out_spec = pl.BlockSpec((None, ts, n), lambda bi, si: (bi, si, 0))
        out_bytes = 2
    est = (2 * ts * d * 4 + d * n * 2 + 2 * ts * n * out_bytes + ts * d * 2 + ts * n * 4
           + ts * d * 4)
    return pl.pallas_call(
        _norm_proj_kernel,
        grid=(b, s // ts),
        in_specs=[
            pl.BlockSpec((None, ts, d), lambda bi, si: (bi, si, 0)),
            pl.BlockSpec((1, d), lambda bi, si: (0, 0)),
            pl.BlockSpec((None, d, n), lambda bi, si: (layer, 0, 0),
                         pipeline_mode=pl.Buffered(1)),
        ],
        out_specs=out_spec,
        out_shape=out_shape,
        compiler_params=pltpu.CompilerParams(
            dimension_semantics=("parallel", "parallel"),
            vmem_limit_bytes=_vmem_limit(est)),
        name=name,
    )(h, g.reshape(1, d), w)


def _proj_post_kernel(y_ref, w_ref, g_ref, h_ref, o_ref, m_ref, *, glu, n_chunk):
    y = y_ref[...]
    y = y.reshape(h_ref.shape[0], y.shape[-1]).astype(BF16)
    d = h_ref.shape[-1]
    for c in range(d // n_chunk):
        lo = c * n_chunk
        val = jnp.dot(y, w_ref[:, lo:lo + n_chunk], preferred_element_type=F32)
        if glu:
            gate = jnp.dot(y, w_ref[:, d + lo:d + lo + n_chunk], preferred_element_type=F32)
            val = val * jax.nn.sigmoid(gate)
        m_ref[:, lo:lo + n_chunk] = val
    o_ref[...] = h_ref[...] + _rms(m_ref[...], g_ref[...])


def _proj_post(y, w, layer, g, h, *, glu, time_major, name):
    b, s, d = h.shape
    _, k, n = w.shape
    if time_major:
        ts = min(GLU_ROW_TILE, s)
        y_spec = pl.BlockSpec((ts // SUBLANES, None, SUBLANES, k),
                              lambda bi, si: (si, bi, 0, 0))
        y_bytes = 4
    else:
        ts = min(ROW_TILE, s)
        y_spec = pl.BlockSpec((None, ts, k), lambda bi, si: (bi, si, 0))
        y_bytes = 2
    h_spec = pl.BlockSpec((None, ts, d), lambda bi, si: (bi, si, 0))
    n_chunk = 512
    est = (2 * ts * k * y_bytes + ts * k * 2 + k * n * 2 + 4 * ts * d * 4 + ts * d * 4
           + 4 * ts * n_chunk * 4 + ts * d * 4)
    return pl.pallas_call(
        functools.partial(_proj_post_kernel, glu=glu, n_chunk=n_chunk),
        grid=(b, s // ts),
        in_specs=[
            y_spec,
            pl.BlockSpec((None, k, n), lambda bi, si: (layer, 0, 0),
                         pipeline_mode=pl.Buffered(1)),
            pl.BlockSpec((1, d), lambda bi, si: (0, 0)),
            h_spec,
        ],
        out_specs=h_spec,
        out_shape=jax.ShapeDtypeStruct(h.shape, F32),
        scratch_shapes=[pltpu.VMEM((ts, d), F32)],
        compiler_params=pltpu.CompilerParams(
            dimension_semantics=("parallel", "parallel"),
            vmem_limit_bytes=_vmem_limit(est)),
        name=name,
    )(y, w, g.reshape(1, d), h)


def _mlp_kernel(h_ref, g1_ref, w1_ref, w2_ref, g2_ref, o_ref, hn_ref, acc_ref):
    f = pl.program_id(1)

    @pl.when(f == 0)
    def _():
        hn_ref[...] = _rms(h_ref[...], g1_ref[...]).astype(BF16)
        acc_ref[...] = jnp.zeros_like(acc_ref)

    a = jnp.dot(hn_ref[...], w1_ref[...], preferred_element_type=F32)
    a = jnp.square(jnp.maximum(a, 0.0)).astype(BF16)
    acc_ref[...] += jnp.dot(a, w2_ref[...], preferred_element_type=F32)

    @pl.when(f == pl.num_programs(1) - 1)
    def _():
        o_ref[...] = h_ref[...] + _rms(acc_ref[...], g2_ref[...])


def _mlp(h, g1, w1, w2, g2, layer, *, name):
    b, s, d = h.shape
    t = b * s
    ff = w1.shape[-1]
    tm = min(ROW_TILE, t)
    tf = min(FF_TILE, ff)
    est = 4 * tm * d * 4 + tm * d * 2 + tm * d * 4 + 4 * d * tf * 2 + 2 * tm * tf * 4 + tm * d * 4
    out = pl.pallas_call(
        _mlp_kernel,
        grid=(t // tm, ff // tf),
        in_specs=[
            pl.BlockSpec((tm, d), lambda m, f: (m, 0)),
            pl.BlockSpec((1, d), lambda m, f: (0, 0)),
            pl.BlockSpec((None, d, tf), lambda m, f: (layer, 0, f)),
            pl.BlockSpec((None, tf, d), lambda m, f: (layer, f, 0)),
            pl.BlockSpec((1, d), lambda m, f: (0, 0)),
        ],
        out_specs=pl.BlockSpec((tm, d), lambda m, f: (m, 0)),
        out_shape=jax.ShapeDtypeStruct((t, d), F32),
        scratch_shapes=[pltpu.VMEM((tm, d), BF16), pltpu.VMEM((tm, d), F32)],
        compiler_params=pltpu.CompilerParams(
            dimension_semantics=("parallel", "arbitrary"),
            vmem_limit_bytes=_vmem_limit(est)),
        name=name,
    )(h.reshape(t, d), g1.reshape(1, d), w1, w2, g2.reshape(1, d))
    return out.reshape(b, s, d)


def _ple_kernel(h_ref, p_ref, wp_ref, wg_ref, o_ref):
    h = h_ref[...]
    gate = jnp.dot(h.astype(BF16), wg_ref[...], preferred_element_type=F32)
    pe = jnp.dot(p_ref[...].astype(BF16), wp_ref[...], preferred_element_type=F32)
    o_ref[...] = h + pe * jax.nn.sigmoid(gate)


def _ple(h, p_all, wp, wg, layer, *, name):
    b, s, d = h.shape
    t = b * s
    pd = p_all.shape[-1]
    tm = min(ROW_TILE, t)
    est = 4 * tm * d * 4 + 2 * tm * pd * 4 + d * d * 2 + pd * d * 2 + tm * d * 2 + 3 * tm * d * 4
    out = pl.pallas_call(
        _ple_kernel,
        grid=(t // tm,),
        in_specs=[
            pl.BlockSpec((tm, d), lambda m: (m, 0)),
            pl.BlockSpec((None, tm, pd), lambda m: (layer, m, 0)),
            pl.BlockSpec((None, pd, d), lambda m: (layer, 0, 0), pipeline_mode=pl.Buffered(1)),
            pl.BlockSpec((None, d, d), lambda m: (layer, 0, 0), pipeline_mode=pl.Buffered(1)),
        ],
        out_specs=pl.BlockSpec((tm, d), lambda m: (m, 0)),
        out_shape=jax.ShapeDtypeStruct((t, d), F32),
        compiler_params=pltpu.CompilerParams(
            dimension_semantics=("parallel",),
            vmem_limit_bytes=_vmem_limit(est)),
        name=name,
    )(h.reshape(t, d), p_all.reshape(p_all.shape[0], t, pd), wp, wg)
    return out.reshape(b, s, d)


def _s5_disc_kernel(are_ref, aim_ref, ls_ref, bre_ref, bim_ref,
                    lbr_ref, lbi_ref, bbr_ref, bbi_ref):
    ar = are_ref[...]
    ai = aim_ref[...]
    dt = jnp.exp(ls_ref[...])
    mag = jnp.exp(ar * dt)
    lbr = mag * jnp.cos(ai * dt)
    lbi = mag * jnp.sin(ai * dt)
    den = ar * ar + ai * ai
    nr = lbr - 1.0
    fr = (nr * ar + lbi * ai) / den
    fi = (lbi * ar - nr * ai) / den
    lbr_ref[...] = lbr
    lbi_ref[...] = lbi
    br = bre_ref[...]
    bi = bim_ref[...]
    bbr_ref[...] = fr[None] * br - fi[None] * bi
    bbi_ref[...] = fr[None] * bi + fi[None] * br


def _s5_discretise(a_re, a_im, log_step, b_re, b_im):
    g, p = a_re.shape
    hg = b_re.shape[-1]
    bt_re = jnp.transpose(b_re, (2, 0, 1))
    bt_im = jnp.transpose(b_im, (2, 0, 1))
    gp = jax.ShapeDtypeStruct((g, p), F32)
    hgp = jax.ShapeDtypeStruct((hg, g, p), F32)
    return pl.pallas_call(
        _s5_disc_kernel,
        out_shape=(gp, gp, hgp, hgp),
        name="s5_discretise",
    )(a_re, a_im, log_step.reshape(g, 1), bt_re, bt_im)


def _slab_block_diag(x):
    eye = jnp.eye(SLAB_GROUPS, dtype=x.dtype)
    ns, gl, a, b = x.shape
    y = jnp.einsum('sgab,gm->sgamb', x, eye)
    return y.reshape(ns, gl * a, gl * b)


def _s5_core_kernel(u_ref, b_ref, c_ref, lam_ref, d_ref, o_ref,
                    bu_ref, xs_ref, st_ref, *, n_batch, n_steps):
    ns = SLAB_STATE
    rs = SUBLANES * n_batch

    @pl.when(pl.program_id(1) == 0)
    def _():
        st_ref[...] = jnp.zeros_like(st_ref)

    lr = jnp.broadcast_to(lam_ref[:, :ns], (n_batch, ns))
    li = jnp.broadcast_to(lam_ref[:, ns:], (n_batch, ns))
    dsk = d_ref[...]

    def tile_rows(k, tl, b0):
        return pl.ds(k * rs + b0 * SUBLANES + tl, SUBLANES, stride=SUBLANES)

    def load_u(k):
        return jnp.concatenate(
            [u_ref[tile_rows(k, tl, b0), :]
             for tl in range(SUBLANES) for b0 in range(0, n_batch, SUBLANES)], axis=0)

    def project_in(k):
        bu_ref[k * rs:(k + 1) * rs, :] = jnp.dot(load_u(k).astype(BF16), b_ref[...],
                                                 preferred_element_type=F32)

    def project_out(k):
        y = jnp.dot(xs_ref[k * rs:(k + 1) * rs, :], c_ref[...], preferred_element_type=F32)
        y = jax.nn.gelu(y + dsk * load_u(k)).astype(o_ref.dtype)
        for tl in range(SUBLANES):
            for b0 in range(0, n_batch, SUBLANES):
                r0 = tl * n_batch + b0
                o_ref[tile_rows(k, tl, b0), :] = y[r0:r0 + SUBLANES, :]

    xr = st_ref[:, :ns]
    xi = st_ref[:, ns:]
    project_in(0)
    for k in range(n_steps // SUBLANES):
        if (k + 1) * SUBLANES < n_steps:
            project_in(k + 1)
        for t in range(k * SUBLANES, (k + 1) * SUBLANES):
            rows = slice(t * n_batch, (t + 1) * n_batch)
            nxr = lr * xr - li * xi + bu_ref[rows, :ns]
            nxi = lr * xi + li * xr + bu_ref[rows, ns:]
            xs_ref[rows, :ns] = nxr.astype(BF16)
            xs_ref[rows, ns:] = nxi.astype(BF16)
            xr, xi = nxr, nxi
        project_out(k)
    st_ref[:, :ns] = xr
    st_ref[:, ns:] = xi


def _s5_core(u_tm, b_exp, c_exp, lam, d_skip, *, n_batch):
    rows, hdim = u_tm.shape
    n_slab = hdim // LANES
    seq = rows // n_batch
    n_steps = min(S5_STEPS, seq)
    r = n_steps * n_batch
    est = (4 * r * LANES * 4 + 4 * LANES * 2 * SLAB_STATE * 2 + r * 2 * SLAB_STATE * 6
           + 2 * r * 2 * SLAB_STATE * 4)
    return pl.pallas_call(
        functools.partial(_s5_core_kernel, n_batch=n_batch, n_steps=n_steps),
        grid=(n_slab, seq // n_steps),
        in_specs=[
            pl.BlockSpec((r, LANES), lambda s, c: (c, s)),
            pl.BlockSpec((None, LANES, 2 * SLAB_STATE), lambda s, c: (s, 0, 0)),
            pl.BlockSpec((None, 2 * SLAB_STATE, LANES), lambda s, c: (s, 0, 0)),
            pl.BlockSpec((None, 1, 2 * SLAB_STATE), lambda s, c: (s, 0, 0)),
            pl.BlockSpec((None, 1, LANES), lambda s, c: (s, 0, 0)),
        ],
        out_specs=pl.BlockSpec((r, LANES), lambda s, c: (c, s)),
        out_shape=jax.ShapeDtypeStruct((rows, hdim), F32),
        scratch_shapes=[
            pltpu.VMEM((r, 2 * SLAB_STATE), F32),
            pltpu.VMEM((r, 2 * SLAB_STATE), BF16),
            pltpu.VMEM((n_batch, 2 * SLAB_STATE), F32),
        ],
        compiler_params=pltpu.CompilerParams(
            dimension_semantics=("parallel", "arbitrary"),
            vmem_limit_bytes=_vmem_limit(est)),
        name="s5_core",
    )(u_tm, b_exp, c_exp, lam, d_skip)


def _s5_mixer(h, g_pre, w_in, a_re, a_im, log_step, b_re, b_im, c_re, c_im, d_skip,
              w_glu, g_post, *, layer):
    b, s, d = h.shape
    hdim = w_in.shape[-1]
    n_slab = hdim // LANES
    lbr, lbi, bbr, bbi = _s5_discretise(a_re, a_im, log_step, b_re, b_im)

    def slabs_b(x):
        x = jnp.transpose(x, (1, 0, 2))
        return x.reshape(n_slab, SLAB_GROUPS, SSM_GROUP, SSM_STATE)

    def slabs_c(x):
        x = jnp.transpose(x, (0, 2, 1))
        return x.reshape(n_slab, SLAB_GROUPS, SSM_STATE, SSM_GROUP)

    b_exp = jnp.concatenate(
        [_slab_block_diag(slabs_b(bbr)), _slab_block_diag(slabs_b(bbi))],
        axis=2).astype(BF16)
    c_exp = jnp.concatenate(
        [_slab_block_diag(slabs_c(c_re)), _slab_block_diag(slabs_c(-c_im))],
        axis=1).astype(BF16)
    lam = jnp.concatenate(
        [lbr.reshape(n_slab, 1, SLAB_STATE), lbi.reshape(n_slab, 1, SLAB_STATE)], axis=2)
    dsk = d_skip.astype(F32).reshape(n_slab, 1, LANES)

    u = _norm_proj(h, g_pre, w_in, layer, time_major=True, name=f"s5_in_proj_{layer}")
    y = _s5_core(u.reshape(s * b, hdim), b_exp, c_exp, lam, dsk, n_batch=b)
    return _proj_post(y.reshape(s // SUBLANES, b, SUBLANES, hdim), w_glu, layer, g_post, h,
                      glu=True, time_major=True, name=f"s5_out_proj_{layer}")


def _attn_kernel(q_ref, k_ref, v_ref, o_ref, qs_ref, acc_ref, c_ref, *, tq, scale):
    i = pl.program_id(1)
    tk = tq
    r = Q_PER_KV * tq
    for hd in range(N_KV_HEADS):
        for g in range(Q_PER_KV):
            lo = (hd * Q_PER_KV + g) * HEAD_DIM
            qs_ref[hd, g * tq:(g + 1) * tq, :] = q_ref[:, lo:lo + HEAD_DIM]

    def span(k0, n_blk, diagonal, first):
        nk = n_blk * tk
        heads = range(N_KV_HEADS)
        cols = [slice(hd * HEAD_DIM, (hd + 1) * HEAD_DIM) for hd in heads]
        kr = lax.broadcasted_iota(jnp.int32, (nk, nk + tk), 0)
        kc = lax.broadcasted_iota(jnp.int32, (nk, nk + tk), 1)
        suffix = jnp.where((kr > kc) | (kc >= nk), 1.0, 0.0).astype(BF16)
        if diagonal:
            row = lax.broadcasted_iota(jnp.int32, (r, tk), 0) & (tq - 1)
            col = lax.broadcasted_iota(jnp.int32, (r, tk), 1)
            strict = col < row

            def mask(x):
                tail = jnp.where(strict, x[:, nk - tk:], 0.0)
                return tail if n_blk == 1 else jnp.concatenate([x[:, :nk - tk], tail], axis=1)
        else:
            def mask(x):
                return x

        z2 = [lax.dot_general(qs_ref[hd], k_ref[pl.ds(k0, nk), cols[hd]],
                              (((1,), (1,)), ((), ())),
                              preferred_element_type=F32) * (scale * LOG2E) for hd in heads]
        log_beta = [jnp.minimum(z, 0.0) - jnp.log2(1.0 + jnp.exp2(-jnp.abs(z))) for z in z2]
        log_keep = [mask(lb - z) for lb, z in zip(log_beta, z2)]
        sums = [jnp.dot(lk.astype(BF16), suffix, preferred_element_type=F32) for lk in log_keep]
        after = [s[:, :nk] for s in sums]
        if not first:
            after = [a + jnp.concatenate([c_ref[hd]] * n_blk, axis=1)
                     for hd, a in zip(heads, after)]
        w = [mask(jnp.exp2(lb + a)) for lb, a in zip(log_beta, after)]
        cmax = None
        for hd in heads:
            pv = jnp.dot(w[hd].astype(BF16), v_ref[pl.ds(k0, nk), cols[hd]],
                         preferred_element_type=F32)
            c_new = sums[hd][:, nk:]
            if not first:
                pv = pv + acc_ref[hd]
                c_new = c_new + c_ref[hd]
            acc_ref[hd] = pv
            c_ref[hd] = c_new
            m = jnp.max(c_new)
            cmax = m if cmax is None else jnp.maximum(cmax, m)
        return cmax

    cmax0 = lax.cond(
        i == 0,
        lambda: span(0, 1, True, True),
        lambda: span(pl.multiple_of((i - 1) * tk, tk), 2, True, True))

    def cond(carry):
        j, cmax = carry
        return jnp.logical_and(j >= 0, cmax > -ATT_SKIP * LOG2E)

    def body(carry):
        j, _ = carry
        return j - 1, span(pl.multiple_of(j * tk, tk), 1, False, False)

    lax.while_loop(cond, body, (jnp.where(i == 0, -1, i - 2), cmax0))

    for hd in range(N_KV_HEADS):
        for g in range(Q_PER_KV):
            lo = (hd * Q_PER_KV + g) * HEAD_DIM
            o_ref[:, lo:lo + HEAD_DIM] = acc_ref[hd, g * tq:(g + 1) * tq, :].astype(o_ref.dtype)


def _attention(q, kv):
    b, s, qw = q.shape
    tq = min(ATT_TILE, s)
    kvw = N_KV_HEADS * HEAD_DIM
    r = Q_PER_KV * tq
    est = (4 * tq * qw * 2 + 4 * s * kvw * 2 + N_KV_HEADS * r * (HEAD_DIM * 6 + tq * 4)
           + 16 * r * tq * 4)
    return pl.pallas_call(
        functools.partial(_attn_kernel, tq=tq, scale=HEAD_DIM ** -0.5),
        grid=(b, s // tq),
        in_specs=[
            pl.BlockSpec((None, tq, qw), lambda bi, qi: (bi, qi, 0)),
            pl.BlockSpec((None, s, kvw), lambda bi, qi: (bi, 0, 0)),
            pl.BlockSpec((None, s, kvw), lambda bi, qi: (bi, 0, 1)),
        ],
        out_specs=pl.BlockSpec((None, tq, qw), lambda bi, qi: (bi, qi, 0)),
        out_shape=jax.ShapeDtypeStruct((b, s, qw), BF16),
        scratch_shapes=[
            pltpu.VMEM((N_KV_HEADS, r, HEAD_DIM), BF16),
            pltpu.VMEM((N_KV_HEADS, r, HEAD_DIM), F32),
            pltpu.VMEM((N_KV_HEADS, r, tq), F32),
        ],
        compiler_params=pltpu.CompilerParams(
            dimension_semantics=("parallel", "arbitrary"),
            vmem_limit_bytes=_vmem_limit(est)),
        name="stick_breaking_attention",
    )(q, kv, kv)


def kernel(x, p, a_norm_pre, a_norm_post, ssm_w_in, ssm_a_re, ssm_a_im, ssm_log_step,
           ssm_b_re, ssm_b_im, ssm_c_re, ssm_c_im, ssm_d, ssm_w_glu,
           kv_norm, w_k, w_v, b_norm_pre, b_norm_post, w_q, w_o,
           mlp_norm_pre, mlp_norm_post, mlp_w1, mlp_w2, ple_w, ple_gate):
    depth = mlp_w1.shape[0]
    n_a = ssm_w_in.shape[0]

    w_in = ssm_w_in.astype(BF16)
    w_glu = ssm_w_glu.astype(BF16)
    w_kv = jnp.concatenate([w_k, w_v], axis=1).astype(BF16)[None]
    wq = w_q.astype(BF16)
    wo = w_o.astype(BF16)
    w1 = mlp_w1.astype(BF16)
    w2 = mlp_w2.astype(BF16)
    wp = ple_w.astype(BF16)
    wg = ple_gate.astype(BF16)

    h = x
    kv = None
    for i in range(depth):
        if i < n_a:
            h = _s5_mixer(h, a_norm_pre[i], w_in, ssm_a_re[i], ssm_a_im[i], ssm_log_step[i],
                          ssm_b_re[i], ssm_b_im[i], ssm_c_re[i], ssm_c_im[i], ssm_d[i],
                          w_glu, a_norm_post[i], layer=i)
        else:
            j = i - n_a
            if j == 0:
                kv = _norm_proj(h, kv_norm, w_kv, 0, time_major=False, name="kv_proj")
            q = _norm_proj(h, b_norm_pre[j], wq, j, time_major=False, name=f"q_proj_{j}")
            o = _attention(q, kv)
            h = _proj_post(o, wo, j, b_norm_post[j], h, glu=False, time_major=False,
                           name=f"attn_out_proj_{j}")
        h = _mlp(h, mlp_norm_pre[i], w1, w2, mlp_norm_post[i], i, name=f"mlp_{i}")
        h = _ple(h, p, wp, wg, i, name=f"ple_{i}")
    return h
```

```python
import functools
import math

import jax
import jax.numpy as jnp
from jax import lax
from jax.experimental import pallas as pl
from jax.experimental.pallas import tpu as pltpu

F32 = jnp.float32
BF16 = jnp.bfloat16

NORM_EPS = 1e-6
SSM_GROUP = 16
SSM_STATE = 64
HEAD_DIM = 128
N_KV_HEADS = 4
Q_PER_KV = 4

LANES = 128
SUBLANES = 8
SLAB_GROUPS = LANES // SSM_GROUP
SLAB_STATE = SLAB_GROUPS * SSM_STATE
V7X_VMEM_BYTES = 64 * 1024 * 1024
VMEM_CAP = V7X_VMEM_BYTES - 8 * 1024 * 1024

ROW_TILE = 512
GLU_ROW_TILE = 512
MLP_ROW_TILE = 1024
FF_TILE = 1024
S5_STEPS = 128
ATT_TILE = 128
ATT_SUBTILES = 2
LOG2E = math.log2(math.e)
ATT_SKIP = 60.0


def _vmem_limit(nbytes):
    return int(min(VMEM_CAP, nbytes * 5 // 4 + (4 << 20)))


def _rms(x, g):
    ms = jnp.mean(x * x, axis=-1, keepdims=True)
    return x * lax.rsqrt(ms + NORM_EPS) * g


def _norm_proj_kernel(x_ref, g_ref, w_ref, o_ref):
    xn = _rms(x_ref[...], g_ref[...]).astype(BF16)
    y = jnp.dot(xn, w_ref[...], preferred_element_type=F32).astype(o_ref.dtype)
    o_ref[...] = y.reshape(o_ref.shape)


def _norm_proj(h, g, w, layer, *, time_major, name):
    b, s, d = h.shape
    n = w.shape[-1]
    ts = min(ROW_TILE, s)
    if time_major:
        out_shape = jax.ShapeDtypeStruct((s // SUBLANES, b, SUBLANES, n), F32)
        out_spec = pl.BlockSpec((ts // SUBLANES, None, SUBLANES, n),
                                lambda bi, si: (si, bi, 0, 0))
        out_bytes = 4
    else:
        out_shape = jax.ShapeDtypeStruct((b, s, n), BF16)
        out_spec = pl.BlockSpec((None, ts, n), lambda bi, si: (bi, si, 0))
        out_bytes = 2
    est = (2 * ts * d * 4 + d * n * 2 + 2 * ts * n * out_bytes + ts * d * 2 + ts * n * 4
           + ts * d * 4)
    return pl.pallas_call(
        _norm_proj_kernel,
        grid=(b, s // ts),
        in_specs=[
            pl.BlockSpec((None, ts, d), lambda bi, si: (bi, si, 0)),
            pl.BlockSpec((1, d), lambda bi, si: (0, 0)),
            pl.BlockSpec((None, d, n), lambda bi, si: (layer, 0, 0),
                         pipeline_mode=pl.Buffered(1)),
        ],
        out_specs=out_spec,
        out_shape=out_shape,
        compiler_params=pltpu.CompilerParams(
            dimension_semantics=("parallel", "parallel"),
            vmem_limit_bytes=_vmem_limit(est)),
        name=name,
    )(h, g.reshape(1, d), w)


def _proj_post_kernel(y_ref, w_ref, g_ref, h_ref, gn_ref, o_ref, on_ref, m_ref, *, glu, n_chunk):
    y = y_ref[...]
    y = y.reshape(h_ref.shape[0], y.shape[-1]).astype(BF16)
    d = h_ref.shape[-1]
    for c in range(d // n_chunk):
        lo = c * n_chunk
        val = jnp.dot(y, w_ref[:, lo:lo + n_chunk], preferred_element_type=F32)
        if glu:
            gate = jnp.dot(y, w_ref[:, d + lo:d + lo + n_chunk], preferred_element_type=F32)
            val = val * jax.nn.sigmoid(gate)
        m_ref[:, lo:lo + n_chunk] = val
    h_new = h_ref[...] + _rms(m_ref[...], g_ref[...])
    o_ref[...] = h_new
    on_ref[...] = _rms(h_new, gn_ref[...]).astype(on_ref.dtype)


def _proj_post(y, w, layer, g, h, g_next, *, glu, time_major, name):
    b, s, d = h.shape
    _, k, n = w.shape
    if time_major:
        ts = min(GLU_ROW_TILE, s)
        y_spec = pl.BlockSpec((ts // SUBLANES, None, SUBLANES, k),
                              lambda bi, si: (si, bi, 0, 0))
        y_bytes = 4
    else:
        ts = min(ROW_TILE, s)
        y_spec = pl.BlockSpec((None, ts, k), lambda bi, si: (bi, si, 0))
        y_bytes = 2
    h_spec = pl.BlockSpec((None, ts, d), lambda bi, si: (bi, si, 0))
    g_spec = pl.BlockSpec((1, d), lambda bi, si: (0, 0))
    n_chunk = 512
    est = (2 * ts * k * y_bytes + ts * k * 2 + k * n * 2 + 4 * ts * d * 4 + 2 * ts * d * 2
           + ts * d * 4 + 4 * ts * n_chunk * 4 + 2 * ts * d * 4)
    return pl.pallas_call(
        functools.partial(_proj_post_kernel, glu=glu, n_chunk=n_chunk),
        grid=(b, s // ts),
        in_specs=[
            y_spec,
            pl.BlockSpec((None, k, n), lambda bi, si: (layer, 0, 0),
                         pipeline_mode=pl.Buffered(1)),
            g_spec,
            h_spec,
            g_spec,
        ],
        out_specs=(h_spec, h_spec),
        out_shape=(jax.ShapeDtypeStruct(h.shape, F32), jax.ShapeDtypeStruct(h.shape, BF16)),
        scratch_shapes=[pltpu.VMEM((ts, d), F32)],
        compiler_params=pltpu.CompilerParams(
            dimension_semantics=("parallel", "parallel"),
            vmem_limit_bytes=_vmem_limit(est)),
        name=name,
    )(y, w, g.reshape(1, d), h, g_next.reshape(1, d))


def _mlp_kernel(hn_ref, w1_ref, w2_ref, o_ref):
    @pl.when(pl.program_id(1) == 0)
    def _():
        o_ref[...] = jnp.zeros_like(o_ref)

    a = jnp.dot(hn_ref[...], w1_ref[...], preferred_element_type=F32)
    a = jnp.square(jnp.maximum(a, 0.0)).astype(BF16)
    o_ref[...] += jnp.dot(a, w2_ref[...], preferred_element_type=F32)


def _mlp(hn, w1, w2, layer, *, name):
    b, s, d = hn.shape
    t = b * s
    ff = w1.shape[-1]
    tm = min(MLP_ROW_TILE, t)
    tf = min(FF_TILE, ff)
    est = 2 * tm * d * 2 + 2 * tm * d * 4 + 4 * d * tf * 2 + tm * tf * 6 + tm * d * 4
    out = pl.pallas_call(
        _mlp_kernel,
        grid=(t // tm, ff // tf),
        in_specs=[
            pl.BlockSpec((tm, d), lambda m, f: (m, 0)),
            pl.BlockSpec((None, d, tf), lambda m, f: (layer, 0, f)),
            pl.BlockSpec((None, tf, d), lambda m, f: (layer, f, 0)),
        ],
        out_specs=pl.BlockSpec((tm, d), lambda m, f: (m, 0)),
        out_shape=jax.ShapeDtypeStruct((t, d), F32),
        compiler_params=pltpu.CompilerParams(
            dimension_semantics=("parallel", "arbitrary"),
            vmem_limit_bytes=_vmem_limit(est)),
        name=name,
    )(hn.reshape(t, d), w1, w2)
    return out.reshape(b, s, d)


def _ple_kernel(h_ref, f_ref, g_ref, p_ref, wp_ref, wg_ref, o_ref):
    h = h_ref[...] + _rms(f_ref[...], g_ref[...])
    gate = jnp.dot(h.astype(BF16), wg_ref[...], preferred_element_type=F32)
    pe = jnp.dot(p_ref[...].astype(BF16), wp_ref[...], preferred_element_type=F32)
    o_ref[...] = h + pe * jax.nn.sigmoid(gate)


def _ple(h, f, g, p_all, wp, wg, layer, *, name):
    b, s, d = h.shape
    t = b * s
    pd = p_all.shape[-1]
    tm = min(ROW_TILE, t)
    est = (6 * tm * d * 4 + 2 * tm * pd * 4 + d * d * 2 + pd * d * 2 + tm * d * 2
           + 4 * tm * d * 4)
    row_spec = pl.BlockSpec((tm, d), lambda m: (m, 0))
    out = pl.pallas_call(
        _ple_kernel,
        grid=(t // tm,),
        in_specs=[
            row_spec,
            row_spec,
            pl.BlockSpec((1, d), lambda m: (0, 0)),
            pl.BlockSpec((None, tm, pd), lambda m: (layer, m, 0)),
            pl.BlockSpec((None, pd, d), lambda m: (layer, 0, 0), pipeline_mode=pl.Buffered(1)),
            pl.BlockSpec((None, d, d), lambda m: (layer, 0, 0), pipeline_mode=pl.Buffered(1)),
        ],
        out_specs=row_spec,
        out_shape=jax.ShapeDtypeStruct((t, d), F32),
        compiler_params=pltpu.CompilerParams(
            dimension_semantics=("parallel",),
            vmem_limit_bytes=_vmem_limit(est)),
        name=name,
    )(h.reshape(t, d), f.reshape(t, d), g.reshape(1, d),
      p_all.reshape(p_all.shape[0], t, pd), wp, wg)
    return out.reshape(b, s, d)


def _s5_disc_kernel(are_ref, aim_ref, ls_ref, bre_ref, bim_ref,
                    lbr_ref, lbi_ref, bbr_ref, bbi_ref):
    ar = are_ref[...]
    ai = aim_ref[...]
    dt = jnp.exp(ls_ref[...])
    mag = jnp.exp(ar * dt)
    lbr = mag * jnp.cos(ai * dt)
    lbi = mag * jnp.sin(ai * dt)
    den = ar * ar + ai * ai
    nr = lbr - 1.0
    fr = (nr * ar + lbi * ai) / den
    fi = (lbi * ar - nr * ai) / den
    lbr_ref[...] = lbr
    lbi_ref[...] = lbi
    br = bre_ref[...]
    bi = bim_ref[...]
    bbr_ref[...] = fr[None] * br - fi[None] * bi
    bbi_ref[...] = fr[None] * bi + fi[None] * br


def _s5_discretise(a_re, a_im, log_step, b_re, b_im):
    g, p = a_re.shape
    hg = b_re.shape[-1]
    bt_re = jnp.transpose(b_re, (2, 0, 1))
    bt_im = jnp.transpose(b_im, (2, 0, 1))
    gp = jax.ShapeDtypeStruct((g, p), F32)
    hgp = jax.ShapeDtypeStruct((hg, g, p), F32)
    return pl.pallas_call(
        _s5_disc_kernel,
        out_shape=(gp, gp, hgp, hgp),
        name="s5_discretise",
    )(a_re, a_im, log_step.reshape(g, 1), bt_re, bt_im)


def _slab_block_diag(x):
    eye = jnp.eye(SLAB_GROUPS, dtype=x.dtype)
    ns, gl, a, b = x.shape
    y = jnp.einsum('sgab,gm->sgamb', x, eye)
    return y.reshape(ns, gl * a, gl * b)


def _s5_core_kernel(u_ref, b_ref, c_ref, lam_ref, d_ref, o_ref,
                    bu_ref, xs_ref, st_ref, *, n_batch, n_steps):
    ns = SLAB_STATE
    rs = SUBLANES * n_batch

    @pl.when(pl.program_id(1) == 0)
    def _():
        st_ref[...] = jnp.zeros_like(st_ref)

    lr = jnp.broadcast_to(lam_ref[:, :ns], (n_batch, ns))
    li = jnp.broadcast_to(lam_ref[:, ns:], (n_batch, ns))
    dsk = d_ref[...]

    def tile_rows(k, tl, b0):
        return pl.ds(k * rs + b0 * SUBLANES + tl, SUBLANES, stride=SUBLANES)

    def load_u(k):
        return jnp.concatenate(
            [u_ref[tile_rows(k, tl, b0), :]
             for tl in range(SUBLANES) for b0 in range(0, n_batch, SUBLANES)], axis=0)

    def project_in(k):
        bu_ref[k * rs:(k + 1) * rs, :] = jnp.dot(load_u(k).astype(BF16), b_ref[...],
                                                 preferred_element_type=F32)

    def project_out(k):
        y = jnp.dot(xs_ref[k * rs:(k + 1) * rs, :], c_ref[...], preferred_element_type=F32)
        y = jax.nn.gelu(y + dsk * load_u(k)).astype(o_ref.dtype)
        for tl in range(SUBLANES):
            for b0 in range(0, n_batch, SUBLANES):
                r0 = tl * n_batch + b0
                o_ref[tile_rows(k, tl, b0), :] = y[r0:r0 + SUBLANES, :]

    xr = st_ref[:, :ns]
    xi = st_ref[:, ns:]
    project_in(0)
    for k in range(n_steps // SUBLANES):
        if (k + 1) * SUBLANES < n_steps:
            project_in(k + 1)
        for t in range(k * SUBLANES, (k + 1) * SUBLANES):
            rows = slice(t * n_batch, (t + 1) * n_batch)
            nxr = lr * xr - li * xi + bu_ref[rows, :ns]
            nxi = lr * xi + li * xr + bu_ref[rows, ns:]
            xs_ref[rows, :ns] = nxr.astype(BF16)
            xs_ref[rows, ns:] = nxi.astype(BF16)
            xr, xi = nxr, nxi
        project_out(k)
    st_ref[:, :ns] = xr
    st_ref[:, ns:] = xi


def _s5_core(u_tm, b_exp, c_exp, lam, d_skip, *, n_batch):
    rows, hdim = u_tm.shape
    n_slab = hdim // LANES
    seq = rows // n_batch
    n_steps = min(S5_STEPS, seq)
    r = n_steps * n_batch
    est = (4 * r * LANES * 4 + 4 * LANES * 2 * SLAB_STATE * 2 + r * 2 * SLAB_STATE * 6
           + 2 * r * 2 * SLAB_STATE * 4)
    return pl.pallas_call(
        functools.partial(_s5_core_kernel, n_batch=n_batch, n_steps=n_steps),
        grid=(n_slab, seq // n_steps),
        in_specs=[
            pl.BlockSpec((r, LANES), lambda s, c: (c, s)),
            pl.BlockSpec((None, LANES, 2 * SLAB_STATE), lambda s, c: (s, 0, 0)),
            pl.BlockSpec((None, 2 * SLAB_STATE, LANES), lambda s, c: (s, 0, 0)),
            pl.BlockSpec((None, 1, 2 * SLAB_STATE), lambda s, c: (s, 0, 0)),
            pl.BlockSpec((None, 1, LANES), lambda s, c: (s, 0, 0)),
        ],
        out_specs=pl.BlockSpec((r, LANES), lambda s, c: (c, s)),
        out_shape=jax.ShapeDtypeStruct((rows, hdim), F32),
        scratch_shapes=[
            pltpu.VMEM((r, 2 * SLAB_STATE), F32),
            pltpu.VMEM((r, 2 * SLAB_STATE), BF16),
            pltpu.VMEM((n_batch, 2 * SLAB_STATE), F32),
        ],
        compiler_params=pltpu.CompilerParams(
            dimension_semantics=("parallel", "arbitrary"),
            vmem_limit_bytes=_vmem_limit(est)),
        name="s5_core",
    )(u_tm, b_exp, c_exp, lam, d_skip)


def _s5_mixer(h, g_pre, w_in, a_re, a_im, log_step, b_re, b_im, c_re, c_im, d_skip,
              w_glu, g_post, g_next, *, layer):
    b, s, d = h.shape
    hdim = w_in.shape[-1]
    n_slab = hdim // LANES
    lbr, lbi, bbr, bbi = _s5_discretise(a_re, a_im, log_step, b_re, b_im)

    def slabs_b(x):
        x = jnp.transpose(x, (1, 0, 2))
        return x.reshape(n_slab, SLAB_GROUPS, SSM_GROUP, SSM_STATE)

    def slabs_c(x):
        x = jnp.transpose(x, (0, 2, 1))
        return x.reshape(n_slab, SLAB_GROUPS, SSM_STATE, SSM_GROUP)

    b_exp = jnp.concatenate(
        [_slab_block_diag(slabs_b(bbr)), _slab_block_diag(slabs_b(bbi))],
        axis=2).astype(BF16)
    c_exp = jnp.concatenate(
        [_slab_block_diag(slabs_c(c_re)), _slab_block_diag(slabs_c(-c_im))],
        axis=1).astype(BF16)
    lam = jnp.concatenate(
        [lbr.reshape(n_slab, 1, SLAB_STATE), lbi.reshape(n_slab, 1, SLAB_STATE)], axis=2)
    dsk = d_skip.astype(F32).reshape(n_slab, 1, LANES)

    u = _norm_proj(h, g_pre, w_in, layer, time_major=True, name=f"s5_in_proj_{layer}")
    y = _s5_core(u.reshape(s * b, hdim), b_exp, c_exp, lam, dsk, n_batch=b)
    return _proj_post(y.reshape(s // SUBLANES, b, SUBLANES, hdim), w_glu, layer, g_post, h,
                      g_next, glu=True, time_major=True, name=f"s5_out_proj_{layer}")


def _attn_kernel(q_ref, k_ref, v_ref, o_ref, qs_ref, acc_ref, c_ref, *, tq, n_sub, scale):
    i = pl.program_id(1)
    tk = tq
    r = Q_PER_KV * tq
    for sub in range(n_sub):
        for hd in range(N_KV_HEADS):
            for g in range(Q_PER_KV):
                lo = (hd * Q_PER_KV + g) * HEAD_DIM
                qs_ref[sub, hd, g * tq:(g + 1) * tq, :] = q_ref[sub * tq:(sub + 1) * tq,
                                                                lo:lo + HEAD_DIM]

    def span(subs, k0s, n_blk, diagonal, first):
        nk = n_blk * tk
        chains = [(sub, hd) for sub in subs for hd in range(N_KV_HEADS)]
        k0 = dict(zip(subs, k0s))
        cols = [slice(hd * HEAD_DIM, (hd + 1) * HEAD_DIM) for hd in range(N_KV_HEADS)]
        kr = lax.broadcasted_iota(jnp.int32, (nk, nk + tk), 0)
        kc = lax.broadcasted_iota(jnp.int32, (nk, nk + tk), 1)
        suffix = jnp.where((kr > kc) | (kc >= nk), 1.0, 0.0).astype(BF16)
        if diagonal:
            row = lax.broadcasted_iota(jnp.int32, (r, tk), 0) & (tq - 1)
            col = lax.broadcasted_iota(jnp.int32, (r, tk), 1)
            strict = col < row

            def mask(x):
                tail = jnp.where(strict, x[:, nk - tk:], 0.0)
                return tail if n_blk == 1 else jnp.concatenate([x[:, :nk - tk], tail], axis=1)
        else:
            def mask(x):
                return x

        z2 = [lax.dot_general(qs_ref[sub, hd], k_ref[pl.ds(k0[sub], nk), cols[hd]],
                              (((1,), (1,)), ((), ())),
                              preferred_element_type=F32) * (scale * LOG2E)
              for sub, hd in chains]
        log_beta = [jnp.minimum(z, 0.0) - jnp.log2(1.0 + jnp.exp2(-jnp.abs(z))) for z in z2]
        log_keep = [mask(lb - z) for lb, z in zip(log_beta, z2)]
        sums = [jnp.dot(lk.astype(BF16), suffix, preferred_element_type=F32) for lk in log_keep]
        after = [s[:, :nk] for s in sums]
        if not first:
            after = [a + jnp.concatenate([c_ref[sub, hd]] * n_blk, axis=1)
                     for (sub, hd), a in zip(chains, after)]
        w = [mask(jnp.exp2(lb + a)) for lb, a in zip(log_beta, after)]
        cmax = {sub: None for sub in subs}
        for n, (sub, hd) in enumerate(chains):
            pv = jnp.dot(w[n].astype(BF16), v_ref[pl.ds(k0[sub], nk), cols[hd]],
                         preferred_element_type=F32)
            c_new = sums[n][:, nk:]
            if not first:
                pv = pv + acc_ref[sub, hd]
                c_new = c_new + c_ref[sub, hd]
            acc_ref[sub, hd] = pv
            c_ref[sub, hd] = c_new
            m = jnp.max(c_new)
            cmax[sub] = m if cmax[sub] is None else jnp.maximum(cmax[sub], m)
        return tuple(cmax[sub] for sub in subs)

    def first_step():
        head = span([0], [0], 1, True, True)
        rest = span(list(range(1, n_sub)), [(s - 1) * tk for s in range(1, n_sub)],
                    2, True, True) if n_sub > 1 else ()
        return head + rest

    def later_step():
        subs = list(range(n_sub))
        return span(subs, [pl.multiple_of((n_sub * i + s - 1) * tk, tk) for s in subs],
                    2, True, True)

    cmax0 = lax.cond(i == 0, first_step, later_step)

    def cond(carry):
        j, cmax = carry
        return jnp.logical_and(j >= 0, cmax > -ATT_SKIP * LOG2E)

    for sub in range(n_sub):
        def body(carry, sub=sub):
            j, _ = carry
            return j - 1, span([sub], [pl.multiple_of(j * tk, tk)], 1, False, False)[0]

        lax.while_loop(cond, body, (n_sub * i + sub - 2, cmax0[sub]))

    for sub in range(n_sub):
        for hd in range(N_KV_HEADS):
            for g in range(Q_PER_KV):
                lo = (hd * Q_PER_KV + g) * HEAD_DIM
                o_ref[sub * tq:(sub + 1) * tq, lo:lo + HEAD_DIM] = (
                    acc_ref[sub, hd, g * tq:(g + 1) * tq, :].astype(o_ref.dtype))


def _attention(q, kv):
    b, s, qw = q.shape
    tq = min(ATT_TILE, s)
    n_sub = min(ATT_SUBTILES, s // tq)
    kvw = N_KV_HEADS * HEAD_DIM
    r = Q_PER_KV * tq
    n_chain = n_sub * N_KV_HEADS
    est = (4 * n_sub * tq * qw * 2 + 4 * s * kvw * 2 + n_chain * r * (HEAD_DIM * 6 + tq * 4)
           + 6 * n_chain * r * tq * 4)
    return pl.pallas_call(
        functools.partial(_attn_kernel, tq=tq, n_sub=n_sub, scale=HEAD_DIM ** -0.5),
        grid=(b, s // (n_sub * tq)),
        in_specs=[
            pl.BlockSpec((None, n_sub * tq, qw), lambda bi, qi: (bi, qi, 0)),
            pl.BlockSpec((None, s, kvw), lambda bi, qi: (bi, 0, 0)),
            pl.BlockSpec((None, s, kvw), lambda bi, qi: (bi, 0, 1)),
        ],
        out_specs=pl.BlockSpec((None, n_sub * tq, qw), lambda bi, qi: (bi, qi, 0)),
        out_shape=jax.ShapeDtypeStruct((b, s, qw), BF16),
        scratch_shapes=[
            pltpu.VMEM((n_sub, N_KV_HEADS, r, HEAD_DIM), BF16),
            pltpu.VMEM((n_sub, N_KV_HEADS, r, HEAD_DIM), F32),
            pltpu.VMEM((n_sub, N_KV_HEADS, r, tq), F32),
        ],
        compiler_params=pltpu.CompilerParams(
            dimension_semantics=("parallel", "arbitrary"),
            vmem_limit_bytes=_vmem_limit(est)),
        name="stick_breaking_attention",
    )(q, kv, kv)


def kernel(x, p, a_norm_pre, a_norm_post, ssm_w_in, ssm_a_re, ssm_a_im, ssm_log_step,
           ssm_b_re, ssm_b_im, ssm_c_re, ssm_c_im, ssm_d, ssm_w_glu,
           kv_norm, w_k, w_v, b_norm_pre, b_norm_post, w_q, w_o,
           mlp_norm_pre, mlp_norm_post, mlp_w1, mlp_w2, ple_w, ple_gate):
    depth = mlp_w1.shape[0]
    n_a = ssm_w_in.shape[0]

    w_in = ssm_w_in.astype(BF16)
    w_glu = ssm_w_glu.astype(BF16)
    w_kv = jnp.concatenate([w_k, w_v], axis=1).astype(BF16)[None]
    wq = w_q.astype(BF16)
    wo = w_o.astype(BF16)
    w1 = mlp_w1.astype(BF16)
    w2 = mlp_w2.astype(BF16)
    wp = ple_w.astype(BF16)
    wg = ple_gate.astype(BF16)

    h = x
    kv = None
    for i in range(depth):
        if i < n_a:
            h, hn = _s5_mixer(h, a_norm_pre[i], w_in, ssm_a_re[i], ssm_a_im[i], ssm_log_step[i],
                              ssm_b_re[i], ssm_b_im[i], ssm_c_re[i], ssm_c_im[i], ssm_d[i],
                              w_glu, a_norm_post[i], mlp_norm_pre[i], layer=i)
        else:
            j = i - n_a
            if j == 0:
                kv = _norm_proj(h, kv_norm, w_kv, 0, time_major=False, name="kv_proj")
            q = _norm_proj(h, b_norm_pre[j], wq, j, time_major=False, name=f"q_proj_{j}")
            o = _attention(q, kv)
            h, hn = _proj_post(o, wo, j, b_norm_post[j], h, mlp_norm_pre[i], glu=False,
                               time_major=False, name=f"attn_out_proj_{j}")
        f = _mlp(hn, w1, w2, i, name=f"mlp_{i}")
        h = _ple(h, f, mlp_norm_post[i], p, wp, wg, i, name=f"ple_{i}")
    return h
```

```python
import functools
import math

import jax
import jax.numpy as jnp
from jax import lax
from jax.experimental import pallas as pl
from jax.experimental.pallas import tpu as pltpu

F32 = jnp.float32
BF16 = jnp.bfloat16

NORM_EPS = 1e-6
SSM_GROUP = 16
SSM_STATE = 64
HEAD_DIM = 128
N_KV_HEADS = 4
Q_PER_KV = 4

LANES = 128
SUBLANES = 8
SLAB_GROUPS = LANES // SSM_GROUP
SLAB_STATE = SLAB_GROUPS * SSM_STATE
V7X_VMEM_BYTES = 64 * 1024 * 1024
VMEM_CAP = V7X_VMEM_BYTES - 8 * 1024 * 1024

ROW_TILE = 512
PROJ_ROW_PARTS = 2
POST_ROW_PARTS = 4
GLU_ROW_TILE = 512
MLP_ROW_TILE = 1024
FF_TILE = 1024
S5_STEPS = 128
ATT_TILE = 128
ATT_SUBTILES = 2
LOG2E = math.log2(math.e)
ATT_SKIP = 32.0


def _vmem_limit(nbytes):
    return int(min(VMEM_CAP, nbytes * 5 // 4 + (4 << 20)))


def _rms(x, g):
    ms = jnp.mean(x * x, axis=-1, keepdims=True)
    return x * lax.rsqrt(ms + NORM_EPS) * g


def _row_parts(n_rows, n_parts):
    step = n_rows // n_parts
    return [slice(i * step, (i + 1) * step) for i in range(n_parts)]


def _norm_proj_kernel(x_ref, g_ref, w_ref, o_ref):
    for rows in _row_parts(x_ref.shape[0], PROJ_ROW_PARTS):
        xn = _rms(x_ref[rows, :], g_ref[...]).astype(BF16)
        y = jnp.dot(xn, w_ref[...], preferred_element_type=F32).astype(o_ref.dtype)
        if len(o_ref.shape) == 2:
            o_ref[rows, :] = y
        else:
            tiles = slice(rows.start // SUBLANES, rows.stop // SUBLANES)
            o_ref[tiles] = y.reshape(-1, SUBLANES, y.shape[-1])


def _norm_proj(h, g, w, layer, *, time_major, name):
    b, s, d = h.shape
    n = w.shape[-1]
    ts = min(ROW_TILE, s)
    if time_major:
        out_shape = jax.ShapeDtypeStruct((s // SUBLANES, b, SUBLANES, n), F32)
        out_spec = pl.BlockSpec((ts // SUBLANES, None, SUBLANES, n),
                                lambda bi, si: (si, bi, 0, 0))
        out_bytes = 4
    else:
        out_shape = jax.ShapeDtypeStruct((b, s, n), BF16)
        out_spec = pl.BlockSpec((None, ts, n), lambda bi, si: (bi, si, 0))
        out_bytes = 2
    est = (2 * ts * d * 4 + d * n * 2 + 2 * ts * n * out_bytes + ts * d * 2 + ts * n * 4
           + ts * d * 4)
    return pl.pallas_call(
        _norm_proj_kernel,
        grid=(b, s // ts),
        in_specs=[
            pl.BlockSpec((None, ts, d), lambda bi, si: (bi, si, 0)),
            pl.BlockSpec((1, d), lambda bi, si: (0, 0)),
            pl.BlockSpec((None, d, n), lambda bi, si: (layer, 0, 0),
                         pipeline_mode=pl.Buffered(1)),
        ],
        out_specs=out_spec,
        out_shape=out_shape,
        compiler_params=pltpu.CompilerParams(
            dimension_semantics=("parallel", "parallel"),
            vmem_limit_bytes=_vmem_limit(est)),
        name=name,
    )(h, g.reshape(1, d), w)


def _proj_post_kernel(y_ref, w_ref, g_ref, h_ref, gn_ref, o_ref, on_ref, m_ref, *, glu, n_chunk):
    d = h_ref.shape[-1]
    for rows in _row_parts(h_ref.shape[0], POST_ROW_PARTS):
        if len(y_ref.shape) == 2:
            y = y_ref[rows, :]
        else:
            y = y_ref[rows.start // SUBLANES:rows.stop // SUBLANES]
            y = y.reshape(rows.stop - rows.start, y.shape[-1])
        y = y.astype(BF16)
        for c in range(d // n_chunk):
            lo = c * n_chunk
            val = jnp.dot(y, w_ref[:, lo:lo + n_chunk], preferred_element_type=F32)
            if glu:
                gate = jnp.dot(y, w_ref[:, d + lo:d + lo + n_chunk],
                               preferred_element_type=F32)
                val = val * jax.nn.sigmoid(gate)
            m_ref[rows, lo:lo + n_chunk] = val
        h_new = h_ref[rows, :] + _rms(m_ref[rows, :], g_ref[...])
        o_ref[rows, :] = h_new
        on_ref[rows, :] = _rms(h_new, gn_ref[...]).astype(on_ref.dtype)


def _proj_post(y, w, layer, g, h, g_next, *, glu, time_major, name):
    b, s, d = h.shape
    _, k, n = w.shape
    if time_major:
        ts = min(GLU_ROW_TILE, s)
        y_spec = pl.BlockSpec((ts // SUBLANES, None, SUBLANES, k),
                              lambda bi, si: (si, bi, 0, 0))
        y_bytes = 4
    else:
        ts = min(ROW_TILE, s)
        y_spec = pl.BlockSpec((None, ts, k), lambda bi, si: (bi, si, 0))
        y_bytes = 2
    h_spec = pl.BlockSpec((None, ts, d), lambda bi, si: (bi, si, 0))
    g_spec = pl.BlockSpec((1, d), lambda bi, si: (0, 0))
    n_chunk = 512
    est = (2 * ts * k * y_bytes + ts * k * 2 + k * n * 2 + 4 * ts * d * 4 + 2 * ts * d * 2
           + ts * d * 4 + 4 * ts * n_chunk * 4 + 2 * ts * d * 4)
    return pl.pallas_call(
        functools.partial(_proj_post_kernel, glu=glu, n_chunk=n_chunk),
        grid=(b, s // ts),
        in_specs=[
            y_spec,
            pl.BlockSpec((None, k, n), lambda bi, si: (layer, 0, 0),
                         pipeline_mode=pl.Buffered(1)),
            g_spec,
            h_spec,
            g_spec,
        ],
        out_specs=(h_spec, h_spec),
        out_shape=(jax.ShapeDtypeStruct(h.shape, F32), jax.ShapeDtypeStruct(h.shape, BF16)),
        scratch_shapes=[pltpu.VMEM((ts, d), F32)],
        compiler_params=pltpu.CompilerParams(
            dimension_semantics=("parallel", "parallel"),
            vmem_limit_bytes=_vmem_limit(est)),
        name=name,
    )(y, w, g.reshape(1, d), h, g_next.reshape(1, d))


def _mlp_kernel(hn_ref, w1_ref, w2_ref, o_ref):
    @pl.when(pl.program_id(1) == 0)
    def _():
        o_ref[...] = jnp.zeros_like(o_ref)

    a = jnp.dot(hn_ref[...], w1_ref[...], preferred_element_type=F32)
    a = jnp.square(jnp.maximum(a, 0.0)).astype(BF16)
    o_ref[...] += jnp.dot(a, w2_ref[...], preferred_element_type=F32)


def _mlp(hn, w1, w2, layer, *, name):
    b, s, d = hn.shape
    t = b * s
    ff = w1.shape[-1]
    tm = min(MLP_ROW_TILE, t)
    tf = min(FF_TILE, ff)
    est = 2 * tm * d * 2 + 2 * tm * d * 4 + 4 * d * tf * 2 + tm * tf * 6 + tm * d * 4
    out = pl.pallas_call(
        _mlp_kernel,
        grid=(t // tm, ff // tf),
        in_specs=[
            pl.BlockSpec((tm, d), lambda m, f: (m, 0)),
            pl.BlockSpec((None, d, tf), lambda m, f: (layer, 0, f)),
            pl.BlockSpec((None, tf, d), lambda m, f: (layer, f, 0)),
        ],
        out_specs=pl.BlockSpec((tm, d), lambda m, f: (m, 0)),
        out_shape=jax.ShapeDtypeStruct((t, d), F32),
        compiler_params=pltpu.CompilerParams(
            dimension_semantics=("parallel", "arbitrary"),
            vmem_limit_bytes=_vmem_limit(est)),
        name=name,
    )(hn.reshape(t, d), w1, w2)
    return out.reshape(b, s, d)


def _ple_kernel(h_ref, f_ref, g_ref, p_ref, wp_ref, wg_ref, o_ref):
    h = h_ref[...] + _rms(f_ref[...], g_ref[...])
    gate = jnp.dot(h.astype(BF16), wg_ref[...], preferred_element_type=F32)
    pe = jnp.dot(p_ref[...].astype(BF16), wp_ref[...], preferred_element_type=F32)
    o_ref[...] = h + pe * jax.nn.sigmoid(gate)


def _ple(h, f, g, p_all, wp, wg, layer, *, name):
    b, s, d = h.shape
    t = b * s
    pd = p_all.shape[-1]
    tm = min(ROW_TILE, t)
    est = (6 * tm * d * 4 + 2 * tm * pd * 4 + d * d * 2 + pd * d * 2 + tm * d * 2
           + 4 * tm * d * 4)
    row_spec = pl.BlockSpec((tm, d), lambda m: (m, 0))
    out = pl.pallas_call(
        _ple_kernel,
        grid=(t // tm,),
        in_specs=[
            row_spec,
            row_spec,
            pl.BlockSpec((1, d), lambda m: (0, 0)),
            pl.BlockSpec((None, tm, pd), lambda m: (layer, m, 0)),
            pl.BlockSpec((None, pd, d), lambda m: (layer, 0, 0), pipeline_mode=pl.Buffered(1)),
            pl.BlockSpec((None, d, d), lambda m: (layer, 0, 0), pipeline_mode=pl.Buffered(1)),
        ],
        out_specs=row_spec,
        out_shape=jax.ShapeDtypeStruct((t, d), F32),
        compiler_params=pltpu.CompilerParams(
            dimension_semantics=("parallel",),
            vmem_limit_bytes=_vmem_limit(est)),
        name=name,
    )(h.reshape(t, d), f.reshape(t, d), g.reshape(1, d),
      p_all.reshape(p_all.shape[0], t, pd), wp, wg)
    return out.reshape(b, s, d)


def _s5_disc_kernel(are_ref, aim_ref, ls_ref, bre_ref, bim_ref,
                    lbr_ref, lbi_ref, bbr_ref, bbi_ref):
    ar = are_ref[...]
    ai = aim_ref[...]
    dt = jnp.exp(ls_ref[...])
    mag = jnp.exp(ar * dt)
    lbr = mag * jnp.cos(ai * dt)
    lbi = mag * jnp.sin(ai * dt)
    den = ar * ar + ai * ai
    nr = lbr - 1.0
    fr = (nr * ar + lbi * ai) / den
    fi = (lbi * ar - nr * ai) / den
    lbr_ref[...] = lbr
    lbi_ref[...] = lbi
    br = bre_ref[...]
    bi = bim_ref[...]
    bbr_ref[...] = fr[None] * br - fi[None] * bi
    bbi_ref[...] = fr[None] * bi + fi[None] * br


def _s5_discretise(a_re, a_im, log_step, b_re, b_im):
    g, p = a_re.shape
    hg = b_re.shape[-1]
    bt_re = jnp.transpose(b_re, (2, 0, 1))
    bt_im = jnp.transpose(b_im, (2, 0, 1))
    gp = jax.ShapeDtypeStruct((g, p), F32)
    hgp = jax.ShapeDtypeStruct((hg, g, p), F32)
    return pl.pallas_call(
        _s5_disc_kernel,
        out_shape=(gp, gp, hgp, hgp),
        name="s5_discretise",
    )(a_re, a_im, log_step.reshape(g, 1), bt_re, bt_im)


def _slab_block_diag(x):
    eye = jnp.eye(SLAB_GROUPS, dtype=x.dtype)
    ns, gl, a, b = x.shape
    y = jnp.einsum('sgab,gm->sgamb', x, eye)
    return y.reshape(ns, gl * a, gl * b)


def _s5_core_kernel(u_ref, b_ref, c_ref, lam_ref, d_ref, o_ref,
                    bu_ref, xs_ref, st_ref, *, n_batch, n_steps):
    ns = SLAB_STATE
    rs = SUBLANES * n_batch

    @pl.when(pl.program_id(1) == 0)
    def _():
        st_ref[...] = jnp.zeros_like(st_ref)

    lr = jnp.broadcast_to(lam_ref[:, :ns], (n_batch, ns))
    li = jnp.broadcast_to(lam_ref[:, ns:], (n_batch, ns))
    dsk = d_ref[...]

    def tile_rows(k, tl, b0):
        return pl.ds(k * rs + b0 * SUBLANES + tl, SUBLANES, stride=SUBLANES)

    def load_u(k):
        return jnp.concatenate(
            [u_ref[tile_rows(k, tl, b0), :]
             for tl in range(SUBLANES) for b0 in range(0, n_batch, SUBLANES)], axis=0)

    def project_in(k):
        bu_ref[k * rs:(k + 1) * rs, :] = jnp.dot(load_u(k).astype(BF16), b_ref[...],
                                                 preferred_element_type=F32)

    def project_out(k):
        y = jnp.dot(xs_ref[k * rs:(k + 1) * rs, :], c_ref[...], preferred_element_type=F32)
        y = jax.nn.gelu(y + dsk * load_u(k)).astype(o_ref.dtype)
        for tl in range(SUBLANES):
            for b0 in range(0, n_batch, SUBLANES):
                r0 = tl * n_batch + b0
                o_ref[tile_rows(k, tl, b0), :] = y[r0:r0 + SUBLANES, :]

    xr = st_ref[:, :ns]
    xi = st_ref[:, ns:]
    n_pieces = n_steps // SUBLANES
    project_in(0)
    for k in range(n_pieces):
        if k + 1 < n_pieces:
            project_in(k + 1)
        if k > 0:
            project_out(k - 1)
        for t in range(k * SUBLANES, (k + 1) * SUBLANES):
            rows = slice(t * n_batch, (t + 1) * n_batch)
            nxr = lr * xr - li * xi + bu_ref[rows, :ns]
            nxi = lr * xi + li * xr + bu_ref[rows, ns:]
            xs_ref[rows, :ns] = nxr.astype(BF16)
            xs_ref[rows, ns:] = nxi.astype(BF16)
            xr, xi = nxr, nxi
    project_out(n_pieces - 1)
    st_ref[:, :ns] = xr
    st_ref[:, ns:] = xi


def _s5_core(u_tm, b_exp, c_exp, lam, d_skip, *, n_batch):
    rows, hdim = u_tm.shape
    n_slab = hdim // LANES
    seq = rows // n_batch
    n_steps = min(S5_STEPS, seq)
    r = n_steps * n_batch
    est = (4 * r * LANES * 4 + 4 * LANES * 2 * SLAB_STATE * 2 + r * 2 * SLAB_STATE * 6
           + 2 * r * 2 * SLAB_STATE * 4)
    return pl.pallas_call(
        functools.partial(_s5_core_kernel, n_batch=n_batch, n_steps=n_steps),
        grid=(n_slab, seq // n_steps),
        in_specs=[
            pl.BlockSpec((r, LANES), lambda s, c: (c, s)),
            pl.BlockSpec((None, LANES, 2 * SLAB_STATE), lambda s, c: (s, 0, 0)),
            pl.BlockSpec((None, 2 * SLAB_STATE, LANES), lambda s, c: (s, 0, 0)),
            pl.BlockSpec((None, 1, 2 * SLAB_STATE), lambda s, c: (s, 0, 0)),
            pl.BlockSpec((None, 1, LANES), lambda s, c: (s, 0, 0)),
        ],
        out_specs=pl.BlockSpec((r, LANES), lambda s, c: (c, s)),
        out_shape=jax.ShapeDtypeStruct((rows, hdim), F32),
        scratch_shapes=[
            pltpu.VMEM((r, 2 * SLAB_STATE), F32),
            pltpu.VMEM((r, 2 * SLAB_STATE), BF16),
            pltpu.VMEM((n_batch, 2 * SLAB_STATE), F32),
        ],
        compiler_params=pltpu.CompilerParams(
            dimension_semantics=("parallel", "arbitrary"),
            vmem_limit_bytes=_vmem_limit(est)),
        name="s5_core",
    )(u_tm, b_exp, c_exp, lam, d_skip)


def _s5_mixer(h, g_pre, w_in, a_re, a_im, log_step, b_re, b_im, c_re, c_im, d_skip,
              w_glu, g_post, g_next, *, layer):
    b, s, d = h.shape
    hdim = w_in.shape[-1]
    n_slab = hdim // LANES
    lbr, lbi, bbr, bbi = _s5_discretise(a_re, a_im, log_step, b_re, b_im)

    def slabs_b(x):
        x = jnp.transpose(x, (1, 0, 2))
        return x.reshape(n_slab, SLAB_GROUPS, SSM_GROUP, SSM_STATE)

    def slabs_c(x):
        x = jnp.transpose(x, (0, 2, 1))
        return x.reshape(n_slab, SLAB_GROUPS, SSM_STATE, SSM_GROUP)

    b_exp = jnp.concatenate(
        [_slab_block_diag(slabs_b(bbr)), _slab_block_diag(slabs_b(bbi))],
        axis=2).astype(BF16)
    c_exp = jnp.concatenate(
        [_slab_block_diag(slabs_c(c_re)), _slab_block_diag(slabs_c(-c_im))],
        axis=1).astype(BF16)
    lam = jnp.concatenate(
        [lbr.reshape(n_slab, 1, SLAB_STATE), lbi.reshape(n_slab, 1, SLAB_STATE)], axis=2)
    dsk = d_skip.astype(F32).reshape(n_slab, 1, LANES)

    u = _norm_proj(h, g_pre, w_in, layer, time_major=True, name=f"s5_in_proj_{layer}")
    y = _s5_core(u.reshape(s * b, hdim), b_exp, c_exp, lam, dsk, n_batch=b)
    return _proj_post(y.reshape(s // SUBLANES, b, SUBLANES, hdim), w_glu, layer, g_post, h,
                      g_next, glu=True, time_major=True, name=f"s5_out_proj_{layer}")


def _attn_kernel(q_ref, k_ref, v_ref, o_ref, qs_ref, acc_ref, c_ref, *, tq, n_sub, scale):
    i = pl.program_id(1)
    tk = tq
    r = Q_PER_KV * tq
    for sub in range(n_sub):
        for hd in range(N_KV_HEADS):
            for g in range(Q_PER_KV):
                lo = (hd * Q_PER_KV + g) * HEAD_DIM
                qs_ref[sub, hd, g * tq:(g + 1) * tq, :] = q_ref[sub * tq:(sub + 1) * tq,
                                                                lo:lo + HEAD_DIM]

    def span(subs, k0s, n_blk, diagonal, first):
        nk = n_blk * tk
        chains = [(sub, hd) for sub in subs for hd in range(N_KV_HEADS)]
        k0 = dict(zip(subs, k0s))
        cols = [slice(hd * HEAD_DIM, (hd + 1) * HEAD_DIM) for hd in range(N_KV_HEADS)]
        kr = lax.broadcasted_iota(jnp.int32, (nk, nk + tk), 0)
        kc = lax.broadcasted_iota(jnp.int32, (nk, nk + tk), 1)
        suffix = jnp.where((kr > kc) | (kc >= nk), 1.0, 0.0).astype(BF16)
        if diagonal:
            row = lax.broadcasted_iota(jnp.int32, (r, tk), 0) & (tq - 1)
            col = lax.broadcasted_iota(jnp.int32, (r, tk), 1)
            strict = col < row

            def mask(x):
                tail = jnp.where(strict, x[:, nk - tk:], 0.0)
                return tail if n_blk == 1 else jnp.concatenate([x[:, :nk - tk], tail], axis=1)
        else:
            def mask(x):
                return x

        z2 = [lax.dot_general(qs_ref[sub, hd], k_ref[pl.ds(k0[sub], nk), cols[hd]],
                              (((1,), (1,)), ((), ())),
                              preferred_element_type=F32) * (scale * LOG2E)
              for sub, hd in chains]
        log_beta = [jnp.minimum(z, 0.0) - jnp.log2(1.0 + jnp.exp2(-jnp.abs(z))) for z in z2]
        log_keep = [mask(lb - z) for lb, z in zip(log_beta, z2)]
        sums = [jnp.dot(lk.astype(BF16), suffix, preferred_element_type=F32) for lk in log_keep]
        after = [s[:, :nk] for s in sums]
        if not first:
            after = [a + jnp.concatenate([c_ref[sub, hd]] * n_blk, axis=1)
                     for (sub, hd), a in zip(chains, after)]
        w = [mask(jnp.exp2(lb + a)) for lb, a in zip(log_beta, after)]
        cmax = {sub: None for sub in subs}
        for n, (sub, hd) in enumerate(chains):
            pv = jnp.dot(w[n].astype(BF16), v_ref[pl.ds(k0[sub], nk), cols[hd]],
                         preferred_element_type=F32)
            c_new = sums[n][:, nk:]
            if not first:
                pv = pv + acc_ref[sub, hd]
                c_new = c_new + c_ref[sub, hd]
            acc_ref[sub, hd] = pv
            c_ref[sub, hd] = c_new
            m = jnp.max(c_new)
            cmax[sub] = m if cmax[sub] is None else jnp.maximum(cmax[sub], m)
        return tuple(cmax[sub] for sub in subs)

    def first_step():
        head = span([0], [0], 1, True, True)
        rest = span(list(range(1, n_sub)), [(s - 1) * tk for s in range(1, n_sub)],
                    2, True, True) if n_sub > 1 else ()
        return head + rest

    def later_step():
        subs = list(range(n_sub))
        return span(subs, [pl.multiple_of((n_sub * i + s - 1) * tk, tk) for s in subs],
                    2, True, True)

    cmax0 = lax.cond(i == 0, first_step, later_step)

    def cond(carry):
        j, cmax = carry
        return jnp.logical_and(j >= 0, cmax > -ATT_SKIP * LOG2E)

    for sub in range(n_sub):
        def body(carry, sub=sub):
            j, _ = carry
            return j - 1, span([sub], [pl.multiple_of(j * tk, tk)], 1, False, False)[0]

        lax.while_loop(cond, body, (n_sub * i + sub - 2, cmax0[sub]))

    for sub in range(n_sub):
        for hd in range(N_KV_HEADS):
            for g in range(Q_PER_KV):
                lo = (hd * Q_PER_KV + g) * HEAD_DIM
                o_ref[sub * tq:(sub + 1) * tq, lo:lo + HEAD_DIM] = (
                    acc_ref[sub, hd, g * tq:(g + 1) * tq, :].astype(o_ref.dtype))


def _attention(q, kv):
    b, s, qw = q.shape
    tq = min(ATT_TILE, s)
    n_sub = min(ATT_SUBTILES, s // tq)
    kvw = N_KV_HEADS * HEAD_DIM
    r = Q_PER_KV * tq
    n_chain = n_sub * N_KV_HEADS
    est = (4 * n_sub * tq * qw * 2 + 4 * s * kvw * 2 + n_chain * r * (HEAD_DIM * 6 + tq * 4)
           + 6 * n_chain * r * tq * 4)
    return pl.pallas_call(
        functools.partial(_attn_kernel, tq=tq, n_sub=n_sub, scale=HEAD_DIM ** -0.5),
        grid=(b, s // (n_sub * tq)),
        in_specs=[
            pl.BlockSpec((None, n_sub * tq, qw), lambda bi, qi: (bi, qi, 0)),
            pl.BlockSpec((None, s, kvw), lambda bi, qi: (bi, 0, 0)),
            pl.BlockSpec((None, s, kvw), lambda bi, qi: (bi, 0, 1)),
        ],
        out_specs=pl.BlockSpec((None, n_sub * tq, qw), lambda bi, qi: (bi, qi, 0)),
        out_shape=jax.ShapeDtypeStruct((b, s, qw), BF16),
        scratch_shapes=[
            pltpu.VMEM((n_sub, N_KV_HEADS, r, HEAD_DIM), BF16),
            pltpu.VMEM((n_sub, N_KV_HEADS, r, HEAD_DIM), F32),
            pltpu.VMEM((n_sub, N_KV_HEADS, r, tq), F32),
        ],
        compiler_params=pltpu.CompilerParams(
            dimension_semantics=("parallel", "arbitrary"),
            vmem_limit_bytes=_vmem_limit(est)),
        name="stick_breaking_attention",
    )(q, kv, kv)


def kernel(x, p, a_norm_pre, a_norm_post, ssm_w_in, ssm_a_re, ssm_a_im, ssm_log_step,
           ssm_b_re, ssm_b_im, ssm_c_re, ssm_c_im, ssm_d, ssm_w_glu,
           kv_norm, w_k, w_v, b_norm_pre, b_norm_post, w_q, w_o,
           mlp_norm_pre, mlp_norm_post, mlp_w1, mlp_w2, ple_w, ple_gate):
    depth = mlp_w1.shape[0]
    n_a = ssm_w_in.shape[0]

    w_in = ssm_w_in.astype(BF16)
    w_glu = ssm_w_glu.astype(BF16)
    w_kv = jnp.concatenate([w_k, w_v], axis=1).astype(BF16)[None]
    wq = w_q.astype(BF16)
    wo = w_o.astype(BF16)
    w1 = mlp_w1.astype(BF16)
    w2 = mlp_w2.astype(BF16)
    wp = ple_w.astype(BF16)
    wg = ple_gate.astype(BF16)

    h = x
    kv = None
    for i in range(depth):
        if i < n_a:
            h, hn = _s5_mixer(h, a_norm_pre[i], w_in, ssm_a_re[i], ssm_a_im[i], ssm_log_step[i],
                              ssm_b_re[i], ssm_b_im[i], ssm_c_re[i], ssm_c_im[i], ssm_d[i],
                              w_glu, a_norm_post[i], mlp_norm_pre[i], layer=i)
        else:
            j = i - n_a
            if j == 0:
                kv = _norm_proj(h, kv_norm, w_kv, 0, time_major=False, name="kv_proj")
            q = _norm_proj(h, b_norm_pre[j], wq, j, time_major=False, name=f"q_proj_{j}")
            o = _attention(q, kv)
            h, hn = _proj_post(o, wo, j, b_norm_post[j], h, mlp_norm_pre[i], glu=False,
                               time_major=False, name=f"attn_out_proj_{j}")
        f = _mlp(hn, w1, w2, i, name=f"mlp_{i}")
        h = _ple(h, f, mlp_norm_post[i], p, wp, wg, i, name=f"ple_{i}")
    return h
```

```python
import functools
import math

import jax
import jax.numpy as jnp
from jax import lax
from jax.experimental import pallas as pl
from jax.experimental.pallas import tpu as pltpu

F32 = jnp.float32
BF16 = jnp.bfloat16

NORM_EPS = 1e-6
SSM_GROUP = 16
SSM_STATE = 64
HEAD_DIM = 128
N_KV_HEADS = 4
Q_PER_KV = 4

LANES = 128
SUBLANES = 8
SLAB_GROUPS = LANES // SSM_GROUP
SLAB_STATE = SLAB_GROUPS * SSM_STATE
V7X_VMEM_BYTES = 64 * 1024 * 1024
VMEM_CAP = V7X_VMEM_BYTES - 8 * 1024 * 1024

ROW_TILE = 512
PROJ_ROW_TILE = 1024
PROJ_ROW_PARTS = 4
POST_ROW_PARTS = 4
PLE_ROW_PARTS = 2
PLE_COL_CHUNK = 512
GLU_ROW_TILE = 512
MLP_ROW_TILE = 1024
FF_TILE = 1024
S5_STEPS = 128
ATT_TILE = 128
ATT_SUBTILES = 2
ATT_GROUP = 1
LOG2E = math.log2(math.e)
ATT_SKIP = 32.0


def _vmem_limit(nbytes):
    return int(min(VMEM_CAP, nbytes * 5 // 4 + (4 << 20)))


def _rms(x, g):
    ms = jnp.mean(x * x, axis=-1, keepdims=True)
    return x * lax.rsqrt(ms + NORM_EPS) * g


def _row_parts(n_rows, n_parts):
    step = n_rows // n_parts
    return [slice(i * step, (i + 1) * step) for i in range(n_parts)]


def _norm_proj_kernel(x_ref, g_ref, w_ref, o_ref):
    for rows in _row_parts(x_ref.shape[0], PROJ_ROW_PARTS):
        xn = _rms(x_ref[rows, :], g_ref[...]).astype(BF16)
        y = jnp.dot(xn, w_ref[...], preferred_element_type=F32).astype(o_ref.dtype)
        if len(o_ref.shape) == 2:
            o_ref[rows, :] = y
        else:
            tiles = slice(rows.start // SUBLANES, rows.stop // SUBLANES)
            o_ref[tiles] = y.reshape(-1, SUBLANES, y.shape[-1])


def _norm_proj(h, g, w, layer, *, time_major, name):
    b, s, d = h.shape
    n = w.shape[-1]
    ts = min(PROJ_ROW_TILE, s)
    if time_major:
        out_shape = jax.ShapeDtypeStruct((s // SUBLANES, b, SUBLANES, n), F32)
        out_spec = pl.BlockSpec((ts // SUBLANES, None, SUBLANES, n),
                                lambda bi, si: (si, bi, 0, 0))
        out_bytes = 4
    else:
        out_shape = jax.ShapeDtypeStruct((b, s, n), BF16)
        out_spec = pl.BlockSpec((None, ts, n), lambda bi, si: (bi, si, 0))
        out_bytes = 2
    est = (2 * ts * d * 4 + d * n * 2 + 2 * ts * n * out_bytes + ts * d * 2 + ts * n * 4
           + ts * d * 4)
    return pl.pallas_call(
        _norm_proj_kernel,
        grid=(b, s // ts),
        in_specs=[
            pl.BlockSpec((None, ts, d), lambda bi, si: (bi, si, 0)),
            pl.BlockSpec((1, d), lambda bi, si: (0, 0)),
            pl.BlockSpec((None, d, n), lambda bi, si: (layer, 0, 0),
                         pipeline_mode=pl.Buffered(1)),
        ],
        out_specs=out_spec,
        out_shape=out_shape,
        compiler_params=pltpu.CompilerParams(
            dimension_semantics=("parallel", "parallel"),
            vmem_limit_bytes=_vmem_limit(est)),
        name=name,
    )(h, g.reshape(1, d), w)


def _proj_post_kernel(y_ref, w_ref, g_ref, h_ref, gn_ref, o_ref, on_ref, m_ref, *, glu, n_chunk):
    d = h_ref.shape[-1]
    for rows in _row_parts(h_ref.shape[0], POST_ROW_PARTS):
        if len(y_ref.shape) == 2:
            y = y_ref[rows, :]
        else:
            y = y_ref[rows.start // SUBLANES:rows.stop // SUBLANES]
            y = y.reshape(rows.stop - rows.start, y.shape[-1])
        y = y.astype(BF16)
        for c in range(d // n_chunk):
            lo = c * n_chunk
            val = jnp.dot(y, w_ref[:, lo:lo + n_chunk], preferred_element_type=F32)
            if glu:
                gate = jnp.dot(y, w_ref[:, d + lo:d + lo + n_chunk],
                               preferred_element_type=F32)
                val = val * jax.nn.sigmoid(gate)
            m_ref[rows, lo:lo + n_chunk] = val
        h_new = h_ref[rows, :] + _rms(m_ref[rows, :], g_ref[...])
        o_ref[rows, :] = h_new
        on_ref[rows, :] = _rms(h_new, gn_ref[...]).astype(on_ref.dtype)


def _proj_post(y, w, layer, g, h, g_next, *, glu, time_major, name):
    b, s, d = h.shape
    _, k, n = w.shape
    if time_major:
        ts = min(GLU_ROW_TILE, s)
        y_spec = pl.BlockSpec((ts // SUBLANES, None, SUBLANES, k),
                              lambda bi, si: (si, bi, 0, 0))
        y_bytes = 4
    else:
        ts = min(ROW_TILE, s)
        y_spec = pl.BlockSpec((None, ts, k), lambda bi, si: (bi, si, 0))
        y_bytes = 2
    h_spec = pl.BlockSpec((None, ts, d), lambda bi, si: (bi, si, 0))
    g_spec = pl.BlockSpec((1, d), lambda bi, si: (0, 0))
    n_chunk = 512
    est = (2 * ts * k * y_bytes + ts * k * 2 + k * n * 2 + 4 * ts * d * 4 + 2 * ts * d * 2
           + ts * d * 4 + 4 * ts * n_chunk * 4 + 2 * ts * d * 4)
    return pl.pallas_call(
        functools.partial(_proj_post_kernel, glu=glu, n_chunk=n_chunk),
        grid=(b, s // ts),
        in_specs=[
            y_spec,
            pl.BlockSpec((None, k, n), lambda bi, si: (layer, 0, 0),
                         pipeline_mode=pl.Buffered(1)),
            g_spec,
            h_spec,
            g_spec,
        ],
        out_specs=(h_spec, h_spec),
        out_shape=(jax.ShapeDtypeStruct(h.shape, F32), jax.ShapeDtypeStruct(h.shape, BF16)),
        scratch_shapes=[pltpu.VMEM((ts, d), F32)],
        compiler_params=pltpu.CompilerParams(
            dimension_semantics=("parallel", "parallel"),
            vmem_limit_bytes=_vmem_limit(est)),
        name=name,
    )(y, w, g.reshape(1, d), h, g_next.reshape(1, d))


def _mlp_kernel(hn_ref, w1_ref, w2_ref, o_ref):
    def partial_sum():
        a = jnp.dot(hn_ref[...], w1_ref[...], preferred_element_type=F32)
        a = jnp.square(jnp.maximum(a, 0.0)).astype(BF16)
        return jnp.dot(a, w2_ref[...], preferred_element_type=F32)

    @pl.when(pl.program_id(1) == 0)
    def _():
        o_ref[...] = partial_sum()

    @pl.when(pl.program_id(1) > 0)
    def _():
        o_ref[...] += partial_sum()


def _mlp(hn, w1, w2, layer, *, name):
    b, s, d = hn.shape
    t = b * s
    ff = w1.shape[-1]
    tm = min(MLP_ROW_TILE, t)
    tf = min(FF_TILE, ff)
    est = 2 * tm * d * 2 + 2 * tm * d * 4 + 4 * d * tf * 2 + tm * tf * 6 + tm * d * 4
    out = pl.pallas_call(
        _mlp_kernel,
        grid=(t // tm, ff // tf),
        in_specs=[
            pl.BlockSpec((tm, d), lambda m, f: (m, 0)),
            pl.BlockSpec((None, d, tf), lambda m, f: (layer, 0, f)),
            pl.BlockSpec((None, tf, d), lambda m, f: (layer, f, 0)),
        ],
        out_specs=pl.BlockSpec((tm, d), lambda m, f: (m, 0)),
        out_shape=jax.ShapeDtypeStruct((t, d), F32),
        compiler_params=pltpu.CompilerParams(
            dimension_semantics=("parallel", "arbitrary"),
            vmem_limit_bytes=_vmem_limit(est)),
        name=name,
    )(hn.reshape(t, d), w1, w2)
    return out.reshape(b, s, d)


def _ple_kernel(h_ref, f_ref, g_ref, p_ref, wp_ref, wg_ref, o_ref):
    d = h_ref.shape[-1]
    for rows in _row_parts(h_ref.shape[0], PLE_ROW_PARTS):
        h = h_ref[rows, :] + _rms(f_ref[rows, :], g_ref[...])
        hb = h.astype(BF16)
        pb = p_ref[rows, :].astype(BF16)
        for lo in range(0, d, PLE_COL_CHUNK):
            cols = slice(lo, lo + PLE_COL_CHUNK)
            gate = jnp.dot(hb, wg_ref[:, cols], preferred_element_type=F32)
            pe = jnp.dot(pb, wp_ref[:, cols], preferred_element_type=F32)
            o_ref[rows, cols] = h[:, cols] + pe * jax.nn.sigmoid(gate)


def _ple(h, f, g, p_all, wp, wg, layer, *, name):
    b, s, d = h.shape
    t = b * s
    pd = p_all.shape[-1]
    tm = min(ROW_TILE, t)
    est = (6 * tm * d * 4 + 2 * tm * pd * 4 + d * d * 2 + pd * d * 2 + tm * d * 2
           + 4 * tm * d * 4)
    row_spec = pl.BlockSpec((tm, d), lambda m: (m, 0))
    out = pl.pallas_call(
        _ple_kernel,
        grid=(t // tm,),
        in_specs=[
            row_spec,
            row_spec,
            pl.BlockSpec((1, d), lambda m: (0, 0)),
            pl.BlockSpec((None, tm, pd), lambda m: (layer, m, 0)),
            pl.BlockSpec((None, pd, d), lambda m: (layer, 0, 0), pipeline_mode=pl.Buffered(1)),
            pl.BlockSpec((None, d, d), lambda m: (layer, 0, 0), pipeline_mode=pl.Buffered(1)),
        ],
        out_specs=row_spec,
        out_shape=jax.ShapeDtypeStruct((t, d), F32),
        compiler_params=pltpu.CompilerParams(
            dimension_semantics=("parallel",),
            vmem_limit_bytes=_vmem_limit(est)),
        name=name,
    )(h.reshape(t, d), f.reshape(t, d), g.reshape(1, d),
      p_all.reshape(p_all.shape[0], t, pd), wp, wg)
    return out.reshape(b, s, d)


def _s5_disc_kernel(are_ref, aim_ref, ls_ref, bre_ref, bim_ref,
                    lbr_ref, lbi_ref, bbr_ref, bbi_ref):
    ar = are_ref[...]
    ai = aim_ref[...]
    dt = jnp.exp(ls_ref[...])
    mag = jnp.exp(ar * dt)
    lbr = mag * jnp.cos(ai * dt)
    lbi = mag * jnp.sin(ai * dt)
    den = ar * ar + ai * ai
    nr = lbr - 1.0
    fr = (nr * ar + lbi * ai) / den
    fi = (lbi * ar - nr * ai) / den
    lbr_ref[...] = lbr
    lbi_ref[...] = lbi
    br = bre_ref[...]
    bi = bim_ref[...]
    bbr_ref[...] = fr[None] * br - fi[None] * bi
    bbi_ref[...] = fr[None] * bi + fi[None] * br


def _s5_discretise(a_re, a_im, log_step, b_re, b_im):
    g, p = a_re.shape
    hg = b_re.shape[-1]
    bt_re = jnp.transpose(b_re, (2, 0, 1))
    bt_im = jnp.transpose(b_im, (2, 0, 1))
    gp = jax.ShapeDtypeStruct((g, p), F32)
    hgp = jax.ShapeDtypeStruct((hg, g, p), F32)
    return pl.pallas_call(
        _s5_disc_kernel,
        out_shape=(gp, gp, hgp, hgp),
        name="s5_discretise",
    )(a_re, a_im, log_step.reshape(g, 1), bt_re, bt_im)


def _slab_block_diag(x):
    eye = jnp.eye(SLAB_GROUPS, dtype=x.dtype)
    ns, gl, a, b = x.shape
    y = jnp.einsum('sgab,gm->sgamb', x, eye)
    return y.reshape(ns, gl * a, gl * b)


def _s5_core_kernel(u_ref, b_ref, c_ref, lam_ref, d_ref, o_ref,
                    bu_ref, xs_ref, st_ref, *, n_batch, n_steps):
    ns = SLAB_STATE
    rs = SUBLANES * n_batch

    @pl.when(pl.program_id(1) == 0)
    def _():
        st_ref[...] = jnp.zeros_like(st_ref)

    lr = jnp.broadcast_to(lam_ref[:, :ns], (n_batch, ns))
    li = jnp.broadcast_to(lam_ref[:, ns:], (n_batch, ns))
    dsk = d_ref[...]

    def tile_rows(k, tl, b0):
        return pl.ds(k * rs + b0 * SUBLANES + tl, SUBLANES, stride=SUBLANES)

    def load_u(k):
        return jnp.concatenate(
            [u_ref[tile_rows(k, tl, b0), :]
             for tl in range(SUBLANES) for b0 in range(0, n_batch, SUBLANES)], axis=0)

    def project_in(k):
        bu_ref[k * rs:(k + 1) * rs, :] = jnp.dot(load_u(k).astype(BF16), b_ref[...],
                                                 preferred_element_type=F32)

    def project_out(k):
        y = jnp.dot(xs_ref[k * rs:(k + 1) * rs, :], c_ref[...], preferred_element_type=F32)
        y = jax.nn.gelu(y + dsk * load_u(k)).astype(o_ref.dtype)
        for tl in range(SUBLANES):
            for b0 in range(0, n_batch, SUBLANES):
                r0 = tl * n_batch + b0
                o_ref[tile_rows(k, tl, b0), :] = y[r0:r0 + SUBLANES, :]

    xr = st_ref[:, :ns]
    xi = st_ref[:, ns:]
    n_pieces = n_steps // SUBLANES
    project_in(0)
    for k in range(n_pieces):
        if k + 1 < n_pieces:
            project_in(k + 1)
        if k > 0:
            project_out(k - 1)
        for t in range(k * SUBLANES, (k + 1) * SUBLANES):
            rows = slice(t * n_batch, (t + 1) * n_batch)
            nxr = lr * xr - li * xi + bu_ref[rows, :ns]
            nxi = lr * xi + li * xr + bu_ref[rows, ns:]
            xs_ref[rows, :ns] = nxr.astype(BF16)
            xs_ref[rows, ns:] = nxi.astype(BF16)
            xr, xi = nxr, nxi
    project_out(n_pieces - 1)
    st_ref[:, :ns] = xr
    st_ref[:, ns:] = xi


def _s5_core(u_tm, b_exp, c_exp, lam, d_skip, *, n_batch):
    rows, hdim = u_tm.shape
    n_slab = hdim // LANES
    seq = rows // n_batch
    n_steps = min(S5_STEPS, seq)
    r = n_steps * n_batch
    est = (4 * r * LANES * 4 + 4 * LANES * 2 * SLAB_STATE * 2 + r * 2 * SLAB_STATE * 6
           + 2 * r * 2 * SLAB_STATE * 4)
    return pl.pallas_call(
        functools.partial(_s5_core_kernel, n_batch=n_batch, n_steps=n_steps),
        grid=(n_slab, seq // n_steps),
        in_specs=[
            pl.BlockSpec((r, LANES), lambda s, c: (c, s)),
            pl.BlockSpec((None, LANES, 2 * SLAB_STATE), lambda s, c: (s, 0, 0)),
            pl.BlockSpec((None, 2 * SLAB_STATE, LANES), lambda s, c: (s, 0, 0)),
            pl.BlockSpec((None, 1, 2 * SLAB_STATE), lambda s, c: (s, 0, 0)),
            pl.BlockSpec((None, 1, LANES), lambda s, c: (s, 0, 0)),
        ],
        out_specs=pl.BlockSpec((r, LANES), lambda s, c: (c, s)),
        out_shape=jax.ShapeDtypeStruct((rows, hdim), F32),
        scratch_shapes=[
            pltpu.VMEM((r, 2 * SLAB_STATE), F32),
            pltpu.VMEM((r, 2 * SLAB_STATE), BF16),
            pltpu.VMEM((n_batch, 2 * SLAB_STATE), F32),
        ],
        compiler_params=pltpu.CompilerParams(
            dimension_semantics=("parallel", "arbitrary"),
            vmem_limit_bytes=_vmem_limit(est)),
        name="s5_core",
    )(u_tm, b_exp, c_exp, lam, d_skip)


def _s5_mixer(h, g_pre, w_in, a_re, a_im, log_step, b_re, b_im, c_re, c_im, d_skip,
              w_glu, g_post, g_next, *, layer):
    b, s, d = h.shape
    hdim = w_in.shape[-1]
    n_slab = hdim // LANES
    lbr, lbi, bbr, bbi = _s5_discretise(a_re, a_im, log_step, b_re, b_im)

    def slabs_b(x):
        x = jnp.transpose(x, (1, 0, 2))
        return x.reshape(n_slab, SLAB_GROUPS, SSM_GROUP, SSM_STATE)

    def slabs_c(x):
        x = jnp.transpose(x, (0, 2, 1))
        return x.reshape(n_slab, SLAB_GROUPS, SSM_STATE, SSM_GROUP)

    b_exp = jnp.concatenate(
        [_slab_block_diag(slabs_b(bbr)), _slab_block_diag(slabs_b(bbi))],
        axis=2).astype(BF16)
    c_exp = jnp.concatenate(
        [_slab_block_diag(slabs_c(c_re)), _slab_block_diag(slabs_c(-c_im))],
        axis=1).astype(BF16)
    lam = jnp.concatenate(
        [lbr.reshape(n_slab, 1, SLAB_STATE), lbi.reshape(n_slab, 1, SLAB_STATE)], axis=2)
    dsk = d_skip.astype(F32).reshape(n_slab, 1, LANES)

    u = _norm_proj(h, g_pre, w_in, layer, time_major=True, name=f"s5_in_proj_{layer}")
    y = _s5_core(u.reshape(s * b, hdim), b_exp, c_exp, lam, dsk, n_batch=b)
    return _proj_post(y.reshape(s // SUBLANES, b, SUBLANES, hdim), w_glu, layer, g_post, h,
                      g_next, glu=True, time_major=True, name=f"s5_out_proj_{layer}")


def _attn_kernel(q_ref, k_ref, v_ref, o_ref, qs_ref, acc_ref, c_ref, *, tq, n_sub, scale):
    i = pl.program_id(1)
    tk = tq
    r = Q_PER_KV * tq
    for sub in range(n_sub):
        for hd in range(N_KV_HEADS):
            for g in range(Q_PER_KV):
                lo = (hd * Q_PER_KV + g) * HEAD_DIM
                qs_ref[sub, hd, g * tq:(g + 1) * tq, :] = q_ref[sub * tq:(sub + 1) * tq,
                                                                lo:lo + HEAD_DIM]

    def span(subs, k0s, n_blk, diagonal, first):
        nk = n_blk * tk
        chains = [(sub, hd) for sub in subs for hd in range(N_KV_HEADS)]
        k0 = dict(zip(subs, k0s))
        cols = [slice(hd * HEAD_DIM, (hd + 1) * HEAD_DIM) for hd in range(N_KV_HEADS)]
        kr = lax.broadcasted_iota(jnp.int32, (nk, nk + tk), 0)
        kc = lax.broadcasted_iota(jnp.int32, (nk, nk + tk), 1)
        suffix = jnp.where((kr > kc) | (kc >= nk), 1.0, 0.0).astype(BF16)
        if diagonal:
            row = lax.broadcasted_iota(jnp.int32, (r, tk), 0) & (tq - 1)
            col = lax.broadcasted_iota(jnp.int32, (r, tk), 1)
            strict = col < row

            def mask(x):
                tail = jnp.where(strict, x[:, nk - tk:], 0.0)
                return tail if n_blk == 1 else jnp.concatenate([x[:, :nk - tk], tail], axis=1)
        else:
            def mask(x):
                return x

        state = {}
        cmax = {sub: None for sub in subs}

        def logits(sub, hd):
            z = lax.dot_general(qs_ref[sub, hd], k_ref[pl.ds(k0[sub], nk), cols[hd]],
                                (((1,), (1,)), ((), ())),
                                preferred_element_type=F32) * (scale * LOG2E)
            log_beta = jnp.minimum(z, 0.0) - jnp.log2(1.0 + jnp.exp2(-jnp.abs(z)))
            state[sub, hd] = (log_beta, mask(log_beta - z).astype(BF16))

        def weights(sub, hd):
            log_beta, log_keep = state[sub, hd]
            sums = jnp.dot(log_keep, suffix, preferred_element_type=F32)
            after = sums[:, :nk]
            c_new = sums[:, nk:]
            if not first:
                after = after + jnp.concatenate([c_ref[sub, hd]] * n_blk, axis=1)
                c_new = c_new + c_ref[sub, hd]
            c_ref[sub, hd] = c_new
            m = jnp.max(c_new)
            cmax[sub] = m if cmax[sub] is None else jnp.maximum(cmax[sub], m)
            state[sub, hd] = mask(jnp.exp2(log_beta + after)).astype(BF16)

        def values(sub, hd):
            pv = jnp.dot(state[sub, hd], v_ref[pl.ds(k0[sub], nk), cols[hd]],
                         preferred_element_type=F32)
            if not first:
                pv = pv + acc_ref[sub, hd]
            acc_ref[sub, hd] = pv

        stages = (logits, weights, values)
        groups = [chains[n:n + ATT_GROUP] for n in range(0, len(chains), ATT_GROUP)]
        for step in range(len(groups) + len(stages) - 1):
            for s, stage in enumerate(stages):
                if 0 <= step - s < len(groups):
                    for chain in groups[step - s]:
                        stage(*chain)
        return tuple(cmax[sub] for sub in subs)

    def first_step():
        head = span([0], [0], 1, True, True)
        rest = span(list(range(1, n_sub)), [(s - 1) * tk for s in range(1, n_sub)],
                    2, True, True) if n_sub > 1 else ()
        return head + rest

    def later_step():
        subs = list(range(n_sub))
        return span(subs, [pl.multiple_of((n_sub * i + s - 1) * tk, tk) for s in subs],
                    2, True, True)

    cmax0 = lax.cond(i == 0, first_step, later_step)

    def cond(carry):
        j, cmax = carry
        return jnp.logical_and(j >= 0, cmax > -ATT_SKIP * LOG2E)

    for sub in range(n_sub):
        def body(carry, sub=sub):
            j, _ = carry
            return j - 1, span([sub], [pl.multiple_of(j * tk, tk)], 1, False, False)[0]

        lax.while_loop(cond, body, (n_sub * i + sub - 2, cmax0[sub]))

    for sub in range(n_sub):
        for hd in range(N_KV_HEADS):
            for g in range(Q_PER_KV):
                lo = (hd * Q_PER_KV + g) * HEAD_DIM
                o_ref[sub * tq:(sub + 1) * tq, lo:lo + HEAD_DIM] = (
                    acc_ref[sub, hd, g * tq:(g + 1) * tq, :].astype(o_ref.dtype))


def _attention(q, kv):
    b, s, qw = q.shape
    tq = min(ATT_TILE, s)
    n_sub = min(ATT_SUBTILES, s // tq)
    kvw = N_KV_HEADS * HEAD_DIM
    r = Q_PER_KV * tq
    n_chain = n_sub * N_KV_HEADS
    est = (4 * n_sub * tq * qw * 2 + 4 * s * kvw * 2 + n_chain * r * (HEAD_DIM * 6 + tq * 4)
           + 6 * n_chain * r * tq * 4)
    return pl.pallas_call(
        functools.partial(_attn_kernel, tq=tq, n_sub=n_sub, scale=HEAD_DIM ** -0.5),
        grid=(b, s // (n_sub * tq)),
        in_specs=[
            pl.BlockSpec((None, n_sub * tq, qw), lambda bi, qi: (bi, qi, 0)),
            pl.BlockSpec((None, s, kvw), lambda bi, qi: (bi, 0, 0)),
            pl.BlockSpec((None, s, kvw), lambda bi, qi: (bi, 0, 1)),
        ],
        out_specs=pl.BlockSpec((None, n_sub * tq, qw), lambda bi, qi: (bi, qi, 0)),
        out_shape=jax.ShapeDtypeStruct((b, s, qw), BF16),
        scratch_shapes=[
            pltpu.VMEM((n_sub, N_KV_HEADS, r, HEAD_DIM), BF16),
            pltpu.VMEM((n_sub, N_KV_HEADS, r, HEAD_DIM), F32),
            pltpu.VMEM((n_sub, N_KV_HEADS, r, tq), F32),
        ],
        compiler_params=pltpu.CompilerParams(
            dimension_semantics=("parallel", "arbitrary"),
            vmem_limit_bytes=_vmem_limit(est)),
        name="stick_breaking_attention",
    )(q, kv, kv)


def kernel(x, p, a_norm_pre, a_norm_post, ssm_w_in, ssm_a_re, ssm_a_im, ssm_log_step,
           ssm_b_re, ssm_b_im, ssm_c_re, ssm_c_im, ssm_d, ssm_w_glu,
           kv_norm, w_k, w_v, b_norm_pre, b_norm_post, w_q, w_o,
           mlp_norm_pre, mlp_norm_post, mlp_w1, mlp_w2, ple_w, ple_gate):
    depth = mlp_w1.shape[0]
    n_a = ssm_w_in.shape[0]

    w_in = ssm_w_in.astype(BF16)
    w_glu = ssm_w_glu.astype(BF16)
    w_kv = jnp.concatenate([w_k, w_v], axis=1).astype(BF16)[None]
    wq = w_q.astype(BF16)
    wo = w_o.astype(BF16)
    w1 = mlp_w1.astype(BF16)
    w2 = mlp_w2.astype(BF16)
    wp = ple_w.astype(BF16)
    wg = ple_gate.astype(BF16)

    h = x
    kv = None
    for i in range(depth):
        if i < n_a:
            h, hn = _s5_mixer(h, a_norm_pre[i], w_in, ssm_a_re[i], ssm_a_im[i], ssm_log_step[i],
                              ssm_b_re[i], ssm_b_im[i], ssm_c_re[i], ssm_c_im[i], ssm_d[i],
                              w_glu, a_norm_post[i], mlp_norm_pre[i], layer=i)
        else:
            j = i - n_a
            if j == 0:
                kv = _norm_proj(h, kv_norm, w_kv, 0, time_major=False, name="kv_proj")
            q = _norm_proj(h, b_norm_pre[j], wq, j, time_major=False, name=f"q_proj_{j}")
            o = _attention(q, kv)
            h, hn = _proj_post(o, wo, j, b_norm_post[j], h, mlp_norm_pre[i], glu=False,
                               time_major=False, name=f"attn_out_proj_{j}")
        f = _mlp(hn, w1, w2, i, name=f"mlp_{i}")
        h = _ple(h, f, mlp_norm_post[i], p, wp, wg, i, name=f"ple_{i}")
    return h
```
